```python
import jax, jax.numpy as jnp
from jax import lax
import numpy as np

D_MODEL = 1024
BATCH = 8
SEQ = 2048
DEPTH = 2
DEC_BATCH = 4
DEC_SEQ = 8192
PAST_LEN = 128

W_CONV = D_MODEL
CONV_HEADS = 8
CONV_K = 3
W_POOL = D_MODEL
POOL_WINDOWS = (2, 4, 8, 16)
N_POOL = len(POOL_WINDOWS)
POOL_DH = W_POOL // N_POOL
W_SGU = D_MODEL
SGU_HEADS = 4
SGU_DH = W_SGU // SGU_HEADS
CHUNK = 128
W_FNET = D_MODEL
FNET_GROUPS = 4
FNET_DH = W_FNET // FNET_GROUPS

N_EVEN = (DEPTH + 1) // 2
N_ODD = DEPTH // 2
EVEN_SPLITS = tuple(int(s) for s in np.cumsum([W_CONV, W_CONV, W_CONV, W_CONV, W_POOL]))
EVEN_IN = 4 * W_CONV + 2 * W_POOL
ODD_SPLITS = tuple(int(s) for s in np.cumsum([W_SGU, W_SGU, W_SGU, W_FNET]))
ODD_IN = 3 * W_SGU + 2 * W_FNET
EPS = 1e-6

kernel_name = "hybrid_conv_pool_sgu_fnet_encoder"


def rmsnorm(x, g):
    x32 = x.astype(jnp.float32)
    y = x32 * lax.rsqrt(jnp.mean(x32 * x32, axis=-1, keepdims=True) + EPS)
    return (y * g.astype(jnp.float32)).astype(x.dtype)


def short_conv_mixer(h, gb, gc, w):
    u = gc * h
    s = u.shape[1]
    pad = CONV_K // 2
    up = jnp.pad(u, ((0, 0), (pad, pad), (0, 0)))
    y = up[:, 0:s] * w[0]
    for k in range(1, CONV_K):
        y = y + up[:, k:k + s] * w[k]
    return gb * y


def multiscale_pool_mixer(v, w_grp, scale):
    bsz, s, _ = v.shape
    v32 = v.astype(jnp.float32)
    cs = jnp.concatenate([jnp.zeros((bsz, 1, W_POOL), jnp.float32), jnp.cumsum(v32, axis=1)], axis=1)
    t = np.arange(s)
    outs = []
    for g, win in enumerate(POOL_WINDOWS):
        lo = np.clip(t - win // 2, 0, s)
        hi = np.clip(t + win - win // 2, 0, s)
        cnt = (hi - lo).astype(np.float32)
        csg = cs[:, :, g * POOL_DH:(g + 1) * POOL_DH]
        mean = (csg[:, hi] - csg[:, lo]) / cnt[None, :, None]
        outs.append(mean - v32[:, :, g * POOL_DH:(g + 1) * POOL_DH])
    d = jnp.stack(outs, axis=2)
    y = jnp.einsum('bsgc,gcd->bsgd', d, w_grp.astype(jnp.float32)).reshape(bsz, s, W_POOL)
    return (y * scale.astype(jnp.float32)).astype(v.dtype)


def chunked_sgu(u, v, g_norm, w_s, b_s):
    bsz, s, _ = v.shape
    v32 = v.astype(jnp.float32).reshape(bsz, s // CHUNK, CHUNK, SGU_HEADS, SGU_DH)
    mu = jnp.mean(v32, axis=-1, keepdims=True)
    var = jnp.mean(jnp.square(v32 - mu), axis=-1, keepdims=True)
    vn = (v32 - mu) * lax.rsqrt(var + EPS) * g_norm.astype(jnp.float32).reshape(SGU_HEADS, SGU_DH)
    mix = jnp.einsum('hqp,bnphd->bnqhd', w_s.astype(jnp.float32), vn)
    mix = mix + jnp.transpose(b_s.astype(jnp.float32))[None, None, :, :, None]
    return u * mix.reshape(bsz, s, W_SGU).astype(u.dtype)


def fourier_mixer(f, w_grp):
    bsz, s, _ = f.shape
    f32 = f.astype(jnp.float32).reshape(bsz, s, FNET_GROUPS, FNET_DH)
    spec = jnp.real(jnp.fft.fft2(f32, axes=(1, 3), norm='ortho')).astype(jnp.float32)
    y = jnp.einsum('bsgc,gcd->bsgd', spec, w_grp.astype(jnp.float32))
    return y.reshape(bsz, s, W_FNET).astype(f.dtype)


def even_layer(x, g, w_in, conv_w, pool_w, pool_scale, w_out):
    h = rmsnorm(x, g)
    p = jnp.einsum('bsd,de->bse', h, w_in)
    a_h, a_b, a_c, a_z, b_v, b_z = jnp.split(p, EVEN_SPLITS, axis=-1)
    y_a = short_conv_mixer(a_h, a_b, a_c, conv_w) * jax.nn.silu(a_z)
    y_b = multiscale_pool_mixer(b_v, pool_w, pool_scale) * jax.nn.silu(b_z)
    y = jnp.concatenate([y_a, y_b], axis=-1)
    return x + jnp.einsum('bse,ed->bsd', y, w_out)


def odd_layer(x, g, w_in, sgu_norm_g, sgu_ws, sgu_bs, fnet_w, w_out):
    h = rmsnorm(x, g)
    p = jnp.einsum('bsd,de->bse', h, w_in)
    c_u, c_v, c_z, d_f, d_z = jnp.split(p, ODD_SPLITS, axis=-1)
    y_c = chunked_sgu(c_u, c_v, sgu_norm_g, sgu_ws, sgu_bs) * jax.nn.silu(c_z)
    y_d = fourier_mixer(d_f, fnet_w) * jax.nn.silu(d_z)
    y = jnp.concatenate([y_c, y_d], axis=-1)
    return x + jnp.einsum('bse,ed->bsd', y, w_out)


def trunk(x, norm_g, ev_w_in, ev_conv_w, ev_pool_w, ev_pool_scale, ev_w_out,
          od_w_in, od_sgu_norm_g, od_sgu_ws, od_sgu_bs, od_fnet_w, od_w_out, final_g):
    for i in range(DEPTH):
        j = i // 2
        if i % 2 == 0:
            x = even_layer(x, norm_g[i], ev_w_in[j], ev_conv_w[j], ev_pool_w[j], ev_pool_scale[j], ev_w_out[j])
        else:
            x = odd_layer(x, norm_g[i], od_w_in[j], od_sgu_norm_g[j], od_sgu_ws[j], od_sgu_bs[j], od_fnet_w[j], od_w_out[j])
    return rmsnorm(x, final_g)


def setup_inputs(seed: int = 0) -> dict:
    key = jax.random.key(seed)
    ks = jax.random.split(key, 16)
    f32 = jnp.float32
    nrm = lambda k, shp: jax.random.normal(k, shp, f32)
    return {
        "x_prompt": nrm(ks[0], (BATCH, SEQ, D_MODEL)),
        "x_sample": nrm(ks[1], (DEC_BATCH, DEC_SEQ, D_MODEL)),
        "norm_g": 1.0 + 0.05 * nrm(ks[2], (DEPTH, D_MODEL)),
        "ev_w_in": nrm(ks[3], (N_EVEN, D_MODEL, EVEN_IN)) * D_MODEL ** -0.5,
        "ev_conv_w": nrm(ks[4], (N_EVEN, CONV_K, W_CONV)) * CONV_K ** -0.5,
        "ev_pool_w": nrm(ks[5], (N_EVEN, N_POOL, POOL_DH, POOL_DH)) * POOL_DH ** -0.5,
        "ev_pool_scale": 1.0 + 0.1 * nrm(ks[6], (N_EVEN, W_POOL)),
        "ev_w_out": nrm(ks[7], (N_EVEN, W_CONV + W_POOL, D_MODEL)) * (0.5 * (W_CONV + W_POOL) ** -0.5),
        "od_w_in": nrm(ks[8], (N_ODD, D_MODEL, ODD_IN)) * D_MODEL ** -0.5,
        "od_sgu_norm_g": 1.0 + 0.05 * nrm(ks[9], (N_ODD, W_SGU)),
        "od_sgu_ws": nrm(ks[10], (N_ODD, SGU_HEADS, CHUNK, CHUNK)) * CHUNK ** -0.5,
        "od_sgu_bs": 0.02 * nrm(ks[11], (N_ODD, SGU_HEADS, CHUNK)),
        "od_fnet_w": nrm(ks[12], (N_ODD, FNET_GROUPS, FNET_DH, FNET_DH)) * FNET_DH ** -0.5,
        "od_w_out": nrm(ks[13], (N_ODD, W_SGU + W_FNET, D_MODEL)) * (0.5 * (W_SGU + W_FNET) ** -0.5),
        "final_g": 1.0 + 0.05 * nrm(ks[14], (D_MODEL,)),
    }


def reference(x_prompt, x_sample, norm_g, ev_w_in, ev_conv_w, ev_pool_w, ev_pool_scale, ev_w_out,
              od_w_in, od_sgu_norm_g, od_sgu_ws, od_sgu_bs, od_fnet_w, od_w_out, final_g):
    y_prompt = trunk(x_prompt, norm_g, ev_w_in, ev_conv_w, ev_pool_w, ev_pool_scale, ev_w_out,
                     od_w_in, od_sgu_norm_g, od_sgu_ws, od_sgu_bs, od_fnet_w, od_w_out, final_g)
    y_sample = trunk(x_sample, norm_g, ev_w_in, ev_conv_w, ev_pool_w, ev_pool_scale, ev_w_out,
                     od_w_in, od_sgu_norm_g, od_sgu_ws, od_sgu_bs, od_fnet_w, od_w_out, final_g)
    return (y_prompt, y_sample)
```

```python
import functools

import numpy as np
import jax
import jax.numpy as jnp
from jax import lax
from jax.experimental import pallas as pl
from jax.experimental.pallas import tpu as pltpu

EPS = 1e-6
CONV_K = 3
POOL_WINDOWS = (2, 4, 8, 16)
GROUP = 256
CHUNK = 128
FFT_N2 = 128

V7X_VMEM_BYTES = 64 * 1024 * 1024
BF16_SUBLANES = 16

HALO = BF16_SUBLANES
SEQ_TILE = 512
VMEM_LIMIT = V7X_VMEM_BYTES - 8 * 1024 * 1024

BF16 = jnp.bfloat16
F32 = jnp.float32


def _dot(a, b):
    return jnp.dot(a, b, preferred_element_type=F32)


def _rms(x, g):
    ms = jnp.mean(x * x, axis=-1, keepdims=True)
    return (x * lax.rsqrt(ms + EPS)) * g


def _silu(z):
    return z * (1.0 / (1.0 + jnp.exp(-z)))


def _const_spec(shape):
    nd = len(shape)
    return pl.BlockSpec(shape, lambda *_: (0,) * nd, pipeline_mode=pl.Buffered(1))


def _params(sem):
    return pltpu.CompilerParams(dimension_semantics=sem, vmem_limit_bytes=VMEM_LIMIT)


def _even_kernel(xp_ref, x_ref, xn_ref, g_ref, win_ref, cw_ref, pw_ref, ps_ref, wout_ref,
                 fg_ref, o_ref, h_scr, y_scr, *, ts, seq, d, final):
    i = pl.program_id(1)
    n = pl.num_programs(1)
    g = g_ref[...]
    x = x_ref[0]
    hp = _rms(xp_ref[0], g)
    hn = _rms(xn_ref[0], g)
    h_scr[0:HALO, :] = jnp.where(i > 0, hp, 0.0).astype(BF16)
    h_scr[HALO:HALO + ts, :] = _rms(x, g).astype(BF16)
    h_scr[HALO + ts:, :] = jnp.where(i < n - 1, hn, 0.0).astype(BF16)

    rows = ts + 2 * HALO
    hh = h_scr[...]
    hm = h_scr[HALO:HALO + ts, :]
    w_conv = d
    nblk = d // GROUP

    for cb in range(nblk):
        c0 = cb * GROUP
        a_h = _dot(hh, win_ref[:, c0:c0 + GROUP])
        a_c = _dot(hh, win_ref[:, 2 * w_conv + c0:2 * w_conv + c0 + GROUP])
        u = a_c * a_h
        cw = cw_ref[:, c0:c0 + GROUP]
        conv = (pltpu.roll(u, 1, axis=0) * cw[0:1] + u * cw[1:2]
                + pltpu.roll(u, rows - 1, axis=0) * cw[2:3])
        conv = conv[HALO:HALO + ts]
        a_b = _dot(hm, win_ref[:, w_conv + c0:w_conv + c0 + GROUP])
        a_z = _dot(hm, win_ref[:, 3 * w_conv + c0:3 * w_conv + c0 + GROUP])
        y_scr[:, c0:c0 + GROUP] = ((a_b * conv) * _silu(a_z)).astype(BF16)

    pos = i * ts + lax.broadcasted_iota(jnp.int32, (ts, 1), 0)
    for grp, win in enumerate(POOL_WINDOWS):
        c0 = 4 * w_conv + grp * GROUP
        v = _dot(hh, win_ref[:, c0:c0 + GROUP])
        s = v + pltpu.roll(v, 1, axis=0)
        half = 1
        while 2 * half < win:
            s = pltpu.roll(s, half, axis=0) + pltpu.roll(s, rows - half, axis=0)
            half *= 2
        lo = jnp.maximum(pos - win // 2, 0)
        hi = jnp.minimum(pos + (win - win // 2), seq)
        inv_cnt = 1.0 / (hi - lo).astype(F32)
        dlt = s[HALO:HALO + ts] * inv_cnt - v[HALO:HALO + ts]
        yb = _dot(dlt.astype(BF16), pw_ref[grp])
        yb = yb * ps_ref[:, grp * GROUP:(grp + 1) * GROUP]
        b_z = _dot(hm, win_ref[:, c0 + d:c0 + d + GROUP])
        y_scr[:, w_conv + grp * GROUP:w_conv + (grp + 1) * GROUP] = (yb * _silu(b_z)).astype(BF16)

    out = x + _dot(y_scr[...], wout_ref[...])
    if final:
        out = _rms(out, fg_ref[...])
    o_ref[0] = out


def _even_layer(x, g, w_in, conv_w, pool_w, pool_scale, w_out, final_g, final):
    bsz, seq, d = x.shape
    ts = min(SEQ_TILE, seq)
    nt = seq // ts
    hb = ts // HALO
    nhb = seq // HALO
    kern = functools.partial(_even_kernel, ts=ts, seq=seq, d=d, final=final)
    return pl.pallas_call(
        kern,
        grid=(bsz, nt),
        in_specs=[
            pl.BlockSpec((1, HALO, d), lambda b, i: (b, jnp.maximum(i * hb - 1, 0), 0)),
            pl.BlockSpec((1, ts, d), lambda b, i: (b, i, 0)),
            pl.BlockSpec((1, HALO, d), lambda b, i: (b, jnp.minimum((i + 1) * hb, nhb - 1), 0)),
            _const_spec((1, d)),
            _const_spec(w_in.shape),
            _const_spec(conv_w.shape),
            _const_spec(pool_w.shape),
            _const_spec((1, d)),
            _const_spec(w_out.shape),
            _const_spec((1, d)),
        ],
        out_specs=pl.BlockSpec((1, ts, d), lambda b, i: (b, i, 0)),
        out_shape=jax.ShapeDtypeStruct(x.shape, F32),
        scratch_shapes=[pltpu.VMEM((ts + 2 * HALO, d), BF16),
                        pltpu.VMEM((ts, 2 * d), BF16)],
        compiler_params=_params(("parallel", "arbitrary")),
    )(x, x, x, g.reshape(1, d), w_in, conv_w, pool_w, pool_scale.reshape(1, d), w_out,
      final_g.reshape(1, d))


def _odd_a_kernel(x_ref, g_ref, win_ref, sg_ref, ws_ref, bs_ref, ab_ref, wout_ref,
                  xm_ref, p_ref, q_ref, gate_ref, y_scr, *, ts, d):
    x = x_ref[0]
    hm = _rms(x, g_ref[...]).astype(BF16)
    nblk = d // GROUP
    for hd in range(nblk):
        c0 = hd * GROUP
        v = _dot(hm, win_ref[:, d + c0:d + c0 + GROUP])
        mu = jnp.mean(v, axis=-1, keepdims=True)
        vc = v - mu
        var = jnp.mean(vc * vc, axis=-1, keepdims=True)
        vn = ((vc * lax.rsqrt(var + EPS)) * sg_ref[:, c0:c0 + GROUP]).astype(BF16)
        ws = ws_ref[hd]
        mix = jnp.concatenate(
            [_dot(ws, vn[c * CHUNK:(c + 1) * CHUNK]) + bs_ref[hd] for c in range(ts // CHUNK)],
            axis=0)
        c_u = _dot(hm, win_ref[:, c0:c0 + GROUP])
        c_z = _dot(hm, win_ref[:, 2 * d + c0:2 * d + c0 + GROUP])
        y_scr[:, c0:c0 + GROUP] = ((c_u * mix) * _silu(c_z)).astype(BF16)
        f = _dot(hm, win_ref[:, 3 * d + c0:3 * d + c0 + GROUP])
        pq = _dot(f.astype(BF16), ab_ref[hd])
        p_ref[0, :, c0:c0 + GROUP] = pq[:, :GROUP].astype(BF16)
        q_ref[0, :, c0:c0 + GROUP] = pq[:, GROUP:].astype(BF16)
        d_z = _dot(hm, win_ref[:, 4 * d + c0:4 * d + c0 + GROUP])
        gate_ref[0, :, c0:c0 + GROUP] = _silu(d_z)
    xm_ref[0] = x + _dot(y_scr[...], wout_ref[...])


def _odd_a(x, g, w_in, sgu_g, sgu_ws, sgu_bs, ab, w_out_c):
    bsz, seq, d = x.shape
    ts = min(SEQ_TILE, seq)
    tile = pl.BlockSpec((1, ts, d), lambda b, i: (b, i, 0))
    kern = functools.partial(_odd_a_kernel, ts=ts, d=d)
    return pl.pallas_call(
        kern,
        grid=(bsz, seq // ts),
        in_specs=[tile, _const_spec((1, d)), _const_spec(w_in.shape), _const_spec((1, d)),
                  _const_spec(sgu_ws.shape), _const_spec(sgu_bs.shape), _const_spec(ab.shape),
                  _const_spec(w_out_c.shape)],
        out_specs=[tile, tile, tile, tile],
        out_shape=[jax.ShapeDtypeStruct(x.shape, F32), jax.ShapeDtypeStruct(x.shape, BF16),
                   jax.ShapeDtypeStruct(x.shape, BF16), jax.ShapeDtypeStruct(x.shape, F32)],
        scratch_shapes=[pltpu.VMEM((ts, d), BF16)],
        compiler_params=_params(("parallel", "parallel")),
    )(x, g.reshape(1, d), w_in, sgu_g.reshape(1, d), sgu_ws, sgu_bs, ab, w_out_c)


def _fft1_kernel(p_ref, q_ref, d1p_ref, d1q_ref, yr_ref, yi_ref, *, n1, tb, d):
    for j in range(tb):
        sl = slice(j * d, (j + 1) * d)
        y = (_dot(d1p_ref[j].astype(BF16), p_ref[0, :, sl])
             + _dot(d1q_ref[j].astype(BF16), q_ref[0, :, sl]))
        yr_ref[0, :, sl] = y[:n1].astype(BF16)
        yi_ref[0, :, sl] = y[n1:].astype(BF16)


def _fft_stage1(p, q, d1p, d1q, n1, tb):
    bsz, seq, d = p.shape
    n2 = seq // n1
    pv = p.reshape(bsz, n1, n2 * d)
    qv = q.reshape(bsz, n1, n2 * d)
    blk = pl.BlockSpec((1, n1, tb * d), lambda t, b: (b, 0, t))
    mblk = pl.BlockSpec((tb, 2 * n1, n1), lambda t, b: (t, 0, 0))
    kern = functools.partial(_fft1_kernel, n1=n1, tb=tb, d=d)
    yr, yi = pl.pallas_call(
        kern,
        grid=(n2 // tb, bsz),
        in_specs=[blk, blk, mblk, mblk],
        out_specs=[blk, blk],
        out_shape=[jax.ShapeDtypeStruct(pv.shape, BF16)] * 2,
        compiler_params=_params(("parallel", "parallel")),
    )(pv, qv, d1p, d1q)
    return yr, yi


def _odd_c_kernel(yr_ref, yi_ref, xm_ref, gate_ref, d2_ref, wout_ref, fg_ref, o_ref, y_scr,
                  *, n2, kb, d, final):
    d2 = d2_ref[...].astype(BF16)
    for j in range(kb):
        rs = slice(j * n2, (j + 1) * n2)
        spec = _dot(d2[:, :n2], yr_ref[0, rs, :]) + _dot(d2[:, n2:], yi_ref[0, rs, :])
        y_scr[rs, :] = (spec * gate_ref[0, :, j * d:(j + 1) * d]).astype(BF16)
    upd = _dot(y_scr[...], wout_ref[...])
    for j in range(kb):
        out = xm_ref[0, :, j * d:(j + 1) * d] + upd[j * n2:(j + 1) * n2]
        if final:
            out = _rms(out, fg_ref[...])
        o_ref[0, :, j * d:(j + 1) * d] = out


def _odd_c(yr, yi, xm, gate, d2, w_out_d, final_g, n1, kb, final):
    bsz, seq, d = xm.shape
    n2 = seq // n1
    yrv = yr.reshape(bsz, seq, d)
    yiv = yi.reshape(bsz, seq, d)
    xmv = xm.reshape(bsz, n2, n1 * d)
    gv = gate.reshape(bsz, n2, n1 * d)
    yblk = pl.BlockSpec((1, kb * n2, d), lambda b, k: (b, k, 0))
    tblk = pl.BlockSpec((1, n2, kb * d), lambda b, k: (b, 0, k))
    kern = functools.partial(_odd_c_kernel, n2=n2, kb=kb, d=d, final=final)
    out = pl.pallas_call(
        kern,
        grid=(bsz, n1 // kb),
        in_specs=[yblk, yblk, tblk, tblk, _const_spec(d2.shape), _const_spec(w_out_d.shape),
                  _const_spec((1, d))],
        out_specs=tblk,
        out_shape=jax.ShapeDtypeStruct(xmv.shape, F32),
        scratch_shapes=[pltpu.VMEM((kb * n2, d), BF16)],
        compiler_params=_params(("parallel", "parallel")),
    )(yrv, yiv, xmv, gv, d2, w_out_d, final_g.reshape(1, d))
    return out.reshape(bsz, seq, d)


def _fold_kernel(c_ref, s_ref, w_ref, ab_ref):
    w = w_ref[0]
    a = jnp.dot(c_ref[...], w, preferred_element_type=F32, precision=lax.Precision.HIGHEST)
    b = jnp.dot(s_ref[...], w, preferred_element_type=F32, precision=lax.Precision.HIGHEST)
    ab_ref[0, :, :GROUP] = a.astype(BF16)
    ab_ref[0, :, GROUP:] = b.astype(BF16)


def _fold_fnet(fnet_w):
    ng, dh, _ = fnet_w.shape
    ang = 2.0 * np.pi * np.outer(np.arange(dh), np.arange(dh)) / dh
    scale = dh ** -0.5
    cc = jnp.asarray(np.cos(ang) * scale, F32)
    sc = jnp.asarray(np.sin(ang) * scale, F32)
    return pl.pallas_call(
        _fold_kernel,
        grid=(ng,),
        in_specs=[pl.BlockSpec((dh, dh), lambda g: (0, 0)), pl.BlockSpec((dh, dh), lambda g: (0, 0)),
                  pl.BlockSpec((1, dh, dh), lambda g: (g, 0, 0))],
        out_specs=pl.BlockSpec((1, dh, 2 * dh), lambda g: (g, 0, 0)),
        out_shape=jax.ShapeDtypeStruct((ng, dh, 2 * dh), BF16),
    )(cc, sc, fnet_w)


def _fft_matrices(seq, n1):
    n2 = seq // n1
    k1 = np.arange(n1)[None, :, None]
    t1 = np.arange(n1)[None, None, :]
    t2 = np.arange(n2)[:, None, None]
    ang = 2.0 * np.pi * (k1 * (t1 * n2 + t2) % seq) / seq
    c = np.cos(ang) * n1 ** -0.5
    s = np.sin(ang) * n1 ** -0.5
    d1p = np.concatenate([c, -s], axis=1)
    d1q = np.concatenate([-s, -c], axis=1)
    ang2 = 2.0 * np.pi * (np.outer(np.arange(n2), np.arange(n2)) % n2) / n2
    d2 = np.concatenate([np.cos(ang2), np.sin(ang2)], axis=1) * n2 ** -0.5
    return tuple(jnp.asarray(m, F32) for m in (d1p, d1q, d2))


def _odd_layer(x, g, w_in, sgu_g, sgu_ws, sgu_bs, ab, w_out, final_g, final):
    bsz, seq, d = x.shape
    n2 = FFT_N2
    n1 = seq // n2
    d1p, d1q, d2 = _fft_matrices(seq, n1)
    xm, p, q, gate = _odd_a(x, g, w_in, sgu_g, sgu_ws, sgu_bs, ab, w_out[:d])
    yr, yi = _fft_stage1(p, q, d1p, d1q, n1, tb=4)
    return _odd_c(yr, yi, xm, gate, d2, w_out[d:], final_g, n1, kb=4, final=final)


def _trunk(x, norm_g, ev, od, final_g):
    depth = norm_g.shape[0]
    for i in range(depth):
        j = i // 2
        final = i == depth - 1
        if i % 2 == 0:
            w_in, conv_w, pool_w, pool_scale, w_out = (t[j] for t in ev)
            x = _even_layer(x, norm_g[i], w_in, conv_w, pool_w, pool_scale, w_out, final_g, final)
        else:
            w_in, sgu_g, sgu_ws, sgu_bs, ab, w_out = (t[j] for t in od)
            x = _odd_layer(x, norm_g[i], w_in, sgu_g, sgu_ws, sgu_bs, ab, w_out, final_g, final)
    return x


def kernel(x_prompt, x_sample, norm_g, ev_w_in, ev_conv_w, ev_pool_w, ev_pool_scale, ev_w_out,
           od_w_in, od_sgu_norm_g, od_sgu_ws, od_sgu_bs, od_fnet_w, od_w_out, final_g):
    ev = (ev_w_in.astype(BF16), ev_conv_w, ev_pool_w.astype(BF16), ev_pool_scale,
          ev_w_out.astype(BF16))
    ab = jnp.stack([_fold_fnet(od_fnet_w[j]) for j in range(od_fnet_w.shape[0])])
    od = (od_w_in.astype(BF16), od_sgu_norm_g, od_sgu_ws.astype(BF16),
          od_sgu_bs[..., None], ab, od_w_out.astype(BF16))
    y_prompt = _trunk(x_prompt, norm_g, ev, od, final_g)
    y_sample = _trunk(x_sample, norm_g, ev, od, final_g)
    return (y_prompt, y_sample)
```

```python
import functools

import numpy as np
import jax
import jax.numpy as jnp
from jax import lax
from jax.experimental import pallas as pl
from jax.experimental.pallas import tpu as pltpu

EPS = 1e-6
CONV_K = 3
POOL_WINDOWS = (2, 4, 8, 16)
GROUP = 256
CHUNK = 128
FFT_N2 = 128
FFT_LANES = 256

V7X_VMEM_BYTES = 64 * 1024 * 1024
BF16_SUBLANES = 16
WORD_SUBLANES = 8

HALO = BF16_SUBLANES
SEQ_TILE = 512
PQ_TILE = WORD_SUBLANES * FFT_N2
VMEM_LIMIT = V7X_VMEM_BYTES - 8 * 1024 * 1024

BF16 = jnp.bfloat16
F32 = jnp.float32
U32 = jnp.uint32


def _dot(a, b):
    return jnp.dot(a, b, preferred_element_type=F32)


def _rms(x, g):
    ms = jnp.mean(x * x, axis=-1, keepdims=True)
    return (x * lax.rsqrt(ms + EPS)) * g


def _silu(z):
    return z * (1.0 / (1.0 + jnp.exp(-z)))


def _words(x):
    return pltpu.bitcast(x.astype(BF16), U32)


def _rows(w):
    return pltpu.bitcast(w, BF16)


def _const_spec(shape):
    nd = len(shape)
    return pl.BlockSpec(shape, lambda *_: (0,) * nd, pipeline_mode=pl.Buffered(1))


def _params(sem):
    return pltpu.CompilerParams(dimension_semantics=sem, vmem_limit_bytes=VMEM_LIMIT)


def _even_kernel(xp_ref, x_ref, xn_ref, g_ref, win_ref, cw_ref, pw_ref, ps_ref, wout_ref,
                 fg_ref, o_ref, h_scr, y_scr, *, ts, seq, d, final):
    i = pl.program_id(1)
    n = pl.num_programs(1)
    g = g_ref[...]
    x = x_ref[0]
    hp = _rms(xp_ref[0], g)
    hn = _rms(xn_ref[0], g)
    h_scr[0:HALO, :] = jnp.where(i > 0, hp, 0.0).astype(BF16)
    h_scr[HALO:HALO + ts, :] = _rms(x, g).astype(BF16)
    h_scr[HALO + ts:, :] = jnp.where(i < n - 1, hn, 0.0).astype(BF16)

    rows = ts + 2 * HALO
    hh = h_scr[...]
    hm = h_scr[HALO:HALO + ts, :]
    w_conv = d
    nblk = d // GROUP

    for cb in range(nblk):
        c0 = cb * GROUP
        a_h = _dot(hh, win_ref[:, c0:c0 + GROUP])
        a_c = _dot(hh, win_ref[:, 2 * w_conv + c0:2 * w_conv + c0 + GROUP])
        u = a_c * a_h
        cw = cw_ref[:, c0:c0 + GROUP]
        conv = (pltpu.roll(u, 1, axis=0) * cw[0:1] + u * cw[1:2]
                + pltpu.roll(u, rows - 1, axis=0) * cw[2:3])
        conv = conv[HALO:HALO + ts]
        a_b = _dot(hm, win_ref[:, w_conv + c0:w_conv + c0 + GROUP])
        a_z = _dot(hm, win_ref[:, 3 * w_conv + c0:3 * w_conv + c0 + GROUP])
        y_scr[:, c0:c0 + GROUP] = ((a_b * conv) * _silu(a_z)).astype(BF16)

    pos = i * ts + lax.broadcasted_iota(jnp.int32, (ts, 1), 0)
    for grp, win in enumerate(POOL_WINDOWS):
        c0 = 4 * w_conv + grp * GROUP
        v = _dot(hh, win_ref[:, c0:c0 + GROUP])
        s = v + pltpu.roll(v, 1, axis=0)
        half = 1
        while 2 * half < win:
            s = pltpu.roll(s, half, axis=0) + pltpu.roll(s, rows - half, axis=0)
            half *= 2
        lo = jnp.maximum(pos - win // 2, 0)
        hi = jnp.minimum(pos + (win - win // 2), seq)
        inv_cnt = 1.0 / (hi - lo).astype(F32)
        dlt = s[HALO:HALO + ts] * inv_cnt - v[HALO:HALO + ts]
        yb = _dot(dlt.astype(BF16), pw_ref[grp])
        yb = yb * ps_ref[:, grp * GROUP:(grp + 1) * GROUP]
        b_z = _dot(hm, win_ref[:, c0 + d:c0 + d + GROUP])
        y_scr[:, w_conv + grp * GROUP:w_conv + (grp + 1) * GROUP] = (yb * _silu(b_z)).astype(BF16)

    out = x + _dot(y_scr[...], wout_ref[...])
    if final:
        out = _rms(out, fg_ref[...])
    o_ref[0] = out


def _even_layer(x, g, w_in, conv_w, pool_w, pool_scale, w_out, final_g, final):
    bsz, seq, d = x.shape
    ts = min(SEQ_TILE, seq)
    nt = seq // ts
    hb = ts // HALO
    nhb = seq // HALO
    kern = functools.partial(_even_kernel, ts=ts, seq=seq, d=d, final=final)
    return pl.pallas_call(
        kern,
        grid=(bsz, nt),
        in_specs=[
            pl.BlockSpec((1, HALO, d), lambda b, i: (b, jnp.maximum(i * hb - 1, 0), 0)),
            pl.BlockSpec((1, ts, d), lambda b, i: (b, i, 0)),
            pl.BlockSpec((1, HALO, d), lambda b, i: (b, jnp.minimum((i + 1) * hb, nhb - 1), 0)),
            _const_spec((1, d)),
            _const_spec(w_in.shape),
            _const_spec(conv_w.shape),
            _const_spec(pool_w.shape),
            _const_spec((1, d)),
            _const_spec(w_out.shape),
            _const_spec((1, d)),
        ],
        out_specs=pl.BlockSpec((1, ts, d), lambda b, i: (b, i, 0)),
        out_shape=jax.ShapeDtypeStruct(x.shape, F32),
        scratch_shapes=[pltpu.VMEM((ts + 2 * HALO, d), BF16),
                        pltpu.VMEM((ts, 2 * d), BF16)],
        compiler_params=_params(("parallel", "arbitrary")),
    )(x, x, x, g.reshape(1, d), w_in, conv_w, pool_w, pool_scale.reshape(1, d), w_out,
      final_g.reshape(1, d))


def _odd_pq_kernel(x_ref, g_ref, wf_ref, ab_ref, pw_ref, qw_ref, *, d):
    hm = _rms(x_ref[0], g_ref[...]).astype(BF16)
    nchunk = x_ref.shape[1] // FFT_N2
    for grp in range(d // GROUP):
        c0 = grp * GROUP
        f = _dot(hm, wf_ref[:, c0:c0 + GROUP])
        pq = _dot(f.astype(BF16), ab_ref[grp])
        pw = _words(pq[:, :GROUP])
        qw = _words(pq[:, GROUP:])
        hw = FFT_N2 // 2
        for j in range(nchunk):
            pw_ref[0, :, j, c0:c0 + GROUP] = pw[j * hw:(j + 1) * hw]
            qw_ref[0, :, j, c0:c0 + GROUP] = qw[j * hw:(j + 1) * hw]


def _odd_pq(x, g, w_f, ab):
    bsz, seq, d = x.shape
    n1 = seq // FFT_N2
    ts = min(PQ_TILE, seq)
    oblk = pl.BlockSpec((1, FFT_N2 // 2, ts // FFT_N2, d), lambda b, i: (b, 0, i, 0))
    kern = functools.partial(_odd_pq_kernel, d=d)
    wshape = jax.ShapeDtypeStruct((bsz, FFT_N2 // 2, n1, d), U32)
    return pl.pallas_call(
        kern,
        grid=(bsz, seq // ts),
        in_specs=[pl.BlockSpec((1, ts, d), lambda b, i: (b, i, 0)), _const_spec((1, d)),
                  _const_spec(w_f.shape), _const_spec(ab.shape)],
        out_specs=[oblk, oblk],
        out_shape=[wshape, wshape],
        compiler_params=_params(("parallel", "parallel")),
    )(x, g.reshape(1, d), w_f, ab)


def _fft_kernel(pw_ref, qw_ref, m1_ref, m2_ref, sw_ref, y_scr, *, n1):
    nw = FFT_N2 // 2
    for w2 in range(nw):
        rhs = jnp.concatenate([_rows(pw_ref[w2]), _rows(qw_ref[w2])], axis=0)
        y_scr[:, w2, :] = _words(_dot(m1_ref[w2], rhs))
    m2 = m2_ref[...]
    for m in range(n1 // 2):
        rhs = jnp.concatenate([_rows(y_scr[2 * m]), _rows(y_scr[n1 + 2 * m]),
                               _rows(y_scr[2 * m + 1]), _rows(y_scr[n1 + 2 * m + 1])], axis=0)
        sw_ref[:, m, :] = _words(_dot(m2, rhs))


def _fft(pw, qw, m1, m2):
    bsz, nw, n1, d = pw.shape
    lanes = FFT_LANES
    iblk = pl.BlockSpec((None, nw, n1, lanes), lambda b, c: (b, 0, 0, c))
    kern = functools.partial(_fft_kernel, n1=n1)
    sw = pl.pallas_call(
        kern,
        grid=(bsz, d // lanes),
        in_specs=[iblk, iblk, _const_spec(m1.shape), _const_spec(m2.shape)],
        out_specs=pl.BlockSpec((None, FFT_N2, n1 // 2, lanes), lambda b, c: (b, 0, 0, c)),
        out_shape=jax.ShapeDtypeStruct((bsz, FFT_N2, n1 // 2, d), U32),
        scratch_shapes=[pltpu.VMEM((2 * n1, nw, lanes), U32)],
        compiler_params=_params(("parallel", "parallel")),
    )(pw, qw, m1, m2)
    return sw.reshape(bsz, FFT_N2 * n1 // 2, d)


def _odd_main_kernel(x_ref, sw_ref, g_ref, win_ref, sg_ref, ws_ref, bs_ref, wout_ref, fg_ref,
                     o_ref, y_scr, *, ts, d, final):
    x = x_ref[0]
    hm = _rms(x, g_ref[...]).astype(BF16)
    for hd in range(d // GROUP):
        c0 = hd * GROUP
        v = _dot(hm, win_ref[:, d + c0:d + c0 + GROUP])
        mu = jnp.mean(v, axis=-1, keepdims=True)
        vc = v - mu
        var = jnp.mean(vc * vc, axis=-1, keepdims=True)
        vn = ((vc * lax.rsqrt(var + EPS)) * sg_ref[:, c0:c0 + GROUP]).astype(BF16)
        ws = ws_ref[hd]
        mix = jnp.concatenate(
            [_dot(ws, vn[c * CHUNK:(c + 1) * CHUNK]) + bs_ref[hd] for c in range(ts // CHUNK)],
            axis=0)
        c_u = _dot(hm, win_ref[:, c0:c0 + GROUP])
        c_z = _dot(hm, win_ref[:, 2 * d + c0:2 * d + c0 + GROUP])
        y_scr[:, c0:c0 + GROUP] = ((c_u * mix) * _silu(c_z)).astype(BF16)
        spec = _rows(sw_ref[0, :, c0:c0 + GROUP]).astype(F32)
        d_z = _dot(hm, win_ref[:, 3 * d + c0:3 * d + c0 + GROUP])
        y_scr[:, d + c0:d + c0 + GROUP] = (spec * _silu(d_z)).astype(BF16)
    out = x + _dot(y_scr[...], wout_ref[...])
    if final:
        out = _rms(out, fg_ref[...])
    o_ref[0] = out


def _odd_main(x, sw, g, w_main, sgu_g, sgu_ws, sgu_bs, w_out, final_g, final):
    bsz, seq, d = x.shape
    ts = min(SEQ_TILE, seq)
    tile = pl.BlockSpec((1, ts, d), lambda b, i: (b, i, 0))
    kern = functools.partial(_odd_main_kernel, ts=ts, d=d, final=final)
    return pl.pallas_call(
        kern,
        grid=(bsz, seq // ts),
        in_specs=[tile, pl.BlockSpec((1, ts // 2, d), lambda b, i: (b, i, 0)), _const_spec((1, d)),
                  _const_spec(w_main.shape), _const_spec((1, d)), _const_spec(sgu_ws.shape),
                  _const_spec(sgu_bs.shape), _const_spec(w_out.shape), _const_spec((1, d))],
        out_specs=tile,
        out_shape=jax.ShapeDtypeStruct(x.shape, F32),
        scratch_shapes=[pltpu.VMEM((ts, 2 * d), BF16)],
        compiler_params=_params(("parallel", "parallel")),
    )(x, sw, g.reshape(1, d), w_main, sgu_g.reshape(1, d), sgu_ws, sgu_bs, w_out,
      final_g.reshape(1, d))


def _fold_kernel(c_ref, s_ref, w_ref, ab_ref):
    w = w_ref[0]
    a = jnp.dot(c_ref[...], w, preferred_element_type=F32, precision=lax.Precision.HIGHEST)
    b = jnp.dot(s_ref[...], w, preferred_element_type=F32, precision=lax.Precision.HIGHEST)
    ab_ref[0, :, :GROUP] = a.astype(BF16)
    ab_ref[0, :, GROUP:] = b.astype(BF16)


def _fold_fnet(fnet_w):
    ng, dh, _ = fnet_w.shape
    ang = 2.0 * np.pi * np.outer(np.arange(dh), np.arange(dh)) / dh
    scale = dh ** -0.5
    cc = jnp.asarray(np.cos(ang) * scale, F32)
    sc = jnp.asarray(np.sin(ang) * scale, F32)
    return pl.pallas_call(
        _fold_kernel,
        grid=(ng,),
        in_specs=[pl.BlockSpec((dh, dh), lambda g: (0, 0)), pl.BlockSpec((dh, dh), lambda g: (0, 0)),
                  pl.BlockSpec((1, dh, dh), lambda g: (g, 0, 0))],
        out_specs=pl.BlockSpec((1, dh, 2 * dh), lambda g: (g, 0, 0)),
        out_shape=jax.ShapeDtypeStruct((ng, dh, 2 * dh), BF16),
    )(cc, sc, fnet_w)


def _round_kernel(t_ref, o_ref):
    o_ref[...] = t_ref[...].astype(BF16)


def _round_bf16(table):
    blk = pl.BlockSpec((1,) + table.shape[1:], lambda i: (i,) + (0,) * (table.ndim - 1))
    return pl.pallas_call(_round_kernel, grid=(table.shape[0],), in_specs=[blk], out_specs=blk,
                          out_shape=jax.ShapeDtypeStruct(table.shape, BF16))(table)


def _fft_tables(seq, n1):
    n2 = seq // n1
    nw = n2 // 2
    w2 = np.arange(nw)[:, None, None, None]
    e = np.arange(2)[None, None, :, None]
    k1 = np.arange(n1)[None, :, None, None]
    t1 = np.arange(n1)[None, None, None, :]
    ang = 2.0 * np.pi * ((k1 * (t1 * n2 + 2 * w2 + e)) % seq) / seq
    c = np.cos(ang) * n1 ** -0.5
    s = np.sin(ang) * n1 ** -0.5
    m1 = np.zeros((nw, 2, n1, 2, 2, n1, 2))
    for ee in range(2):
        m1[:, 0, :, ee, 0, :, ee] = c[:, :, ee, :]
        m1[:, 0, :, ee, 1, :, ee] = -s[:, :, ee, :]
        m1[:, 1, :, ee, 0, :, ee] = -s[:, :, ee, :]
        m1[:, 1, :, ee, 1, :, ee] = -c[:, :, ee, :]
    m1 = m1.reshape(nw, 4 * n1, 4 * n1)
    ang2 = 2.0 * np.pi * (np.outer(np.arange(n2), np.arange(n2)) % n2) / n2
    m2 = np.zeros((n2, 2, 2, 2, n2))
    for ee in range(2):
        m2[:, ee, ee, 0, :] = np.cos(ang2) * n2 ** -0.5
        m2[:, ee, ee, 1, :] = np.sin(ang2) * n2 ** -0.5
    m2 = m2.reshape(2 * n2, 4 * n2)
    return _round_bf16(jnp.asarray(m1, F32)), _round_bf16(jnp.asarray(m2, F32)[None])[0]


def _odd_layer(x, g, w_in, sgu_g, sgu_ws, sgu_bs, ab, w_out, final_g, final):
    bsz, seq, d = x.shape
    n1 = seq // FFT_N2
    m1, m2 = _fft_tables(seq, n1)
    w_f = w_in[:, 3 * d:4 * d]
    w_main = jnp.concatenate([w_in[:, :3 * d], w_in[:, 4 * d:]], axis=1)
    pw, qw = _odd_pq(x, g, w_f, ab)
    sw = _fft(pw, qw, m1, m2)
    return _odd_main(x, sw, g, w_main, sgu_g, sgu_ws, sgu_bs, w_out, final_g, final)


def _trunk(x, norm_g, ev, od, final_g):
    depth = norm_g.shape[0]
    for i in range(depth):
        j = i // 2
        final = i == depth - 1
        if i % 2 == 0:
            w_in, conv_w, pool_w, pool_scale, w_out = (t[j] for t in ev)
            x = _even_layer(x, norm_g[i], w_in, conv_w, pool_w, pool_scale, w_out, final_g, final)
        else:
            w_in, sgu_g, sgu_ws, sgu_bs, ab, w_out = (t[j] for t in od)
            x = _odd_layer(x, norm_g[i], w_in, sgu_g, sgu_ws, sgu_bs, ab, w_out, final_g, final)
    return x


def kernel(x_prompt, x_sample, norm_g, ev_w_in, ev_conv_w, ev_pool_w, ev_pool_scale, ev_w_out,
           od_w_in, od_sgu_norm_g, od_sgu_ws, od_sgu_bs, od_fnet_w, od_w_out, final_g):
    ev = (ev_w_in.astype(BF16), ev_conv_w, ev_pool_w.astype(BF16), ev_pool_scale,
          ev_w_out.astype(BF16))
    ab = jnp.stack([_fold_fnet(od_fnet_w[j]) for j in range(od_fnet_w.shape[0])])
    od = (od_w_in.astype(BF16), od_sgu_norm_g, od_sgu_ws.astype(BF16),
          od_sgu_bs[..., None], ab, od_w_out.astype(BF16))
    y_prompt = _trunk(x_prompt, norm_g, ev, od, final_g)
    y_sample = _trunk(x_sample, norm_g, ev, od, final_g)
    return (y_prompt, y_sample)
```

```python
import functools

import numpy as np
import jax
import jax.numpy as jnp
from jax import lax
from jax.experimental import pallas as pl
from jax.experimental.pallas import tpu as pltpu

EPS = 1e-6
CONV_K = 3
POOL_WINDOWS = (2, 4, 8, 16)
GROUP = 256
CHUNK = 128
FFT_N2 = 128
FFT_LANES = 256

V7X_VMEM_BYTES = 64 * 1024 * 1024
BF16_SUBLANES = 16
WORD_SUBLANES = 8
MXU_SPLIT_MIN_ROWS = 256

HALO = BF16_SUBLANES
SEQ_TILE = 512
PQ_TILE = WORD_SUBLANES * FFT_N2
VMEM_LIMIT = V7X_VMEM_BYTES - 8 * 1024 * 1024
ROUND_BLOCK_BYTES = 2 * 1024 * 1024

BF16 = jnp.bfloat16
F32 = jnp.float32
U32 = jnp.uint32


def _dot(a, b):
    m = a.shape[0]
    if m >= MXU_SPLIT_MIN_ROWS and m % (2 * BF16_SUBLANES) == 0:
        h = m // 2
        return jnp.concatenate([jnp.dot(a[:h], b, preferred_element_type=F32),
                                jnp.dot(a[h:], b, preferred_element_type=F32)], axis=0)
    return jnp.dot(a, b, preferred_element_type=F32)


def _rms(x, g):
    ms = jnp.mean(x * x, axis=-1, keepdims=True)
    return (x * lax.rsqrt(ms + EPS)) * g


def _silu(z):
    return z * (1.0 / (1.0 + jnp.exp(-z)))


def _words(x):
    return pltpu.bitcast(x.astype(BF16), U32)


def _rows(w):
    return pltpu.bitcast(w, BF16)


def _const_spec(shape):
    nd = len(shape)
    return pl.BlockSpec(shape, lambda *_: (0,) * nd, pipeline_mode=pl.Buffered(1))


def _params(sem):
    return pltpu.CompilerParams(dimension_semantics=sem, vmem_limit_bytes=VMEM_LIMIT)


def _even_kernel(xp_ref, x_ref, xn_ref, g_ref, win_ref, cw_ref, pw_ref, ps_ref, wout_ref,
                 fg_ref, o_ref, h_scr, y_scr, *, ts, seq, d, final):
    i = pl.program_id(1)
    n = pl.num_programs(1)
    g = g_ref[...]
    x = x_ref[0]
    hp = _rms(xp_ref[0], g)
    hn = _rms(xn_ref[0], g)
    h_scr[0:HALO, :] = jnp.where(i > 0, hp, 0.0).astype(BF16)
    h_scr[HALO:HALO + ts, :] = _rms(x, g).astype(BF16)
    h_scr[HALO + ts:, :] = jnp.where(i < n - 1, hn, 0.0).astype(BF16)

    rows = ts + 2 * HALO
    hh = h_scr[...]
    hm = h_scr[HALO:HALO + ts, :]
    w_conv = d
    nblk = d // GROUP

    def conv_dots(cb):
        c0 = cb * GROUP
        return (_dot(hh, win_ref[:, c0:c0 + GROUP]),
                _dot(hh, win_ref[:, 2 * w_conv + c0:2 * w_conv + c0 + GROUP]),
                _dot(hm, win_ref[:, w_conv + c0:w_conv + c0 + GROUP]),
                _dot(hm, win_ref[:, 3 * w_conv + c0:3 * w_conv + c0 + GROUP]))

    def conv_mix(cb, a_h, a_c, a_b, a_z):
        c0 = cb * GROUP
        u = a_c * a_h
        cw = cw_ref[:, c0:c0 + GROUP]
        conv = (pltpu.roll(u, 1, axis=0) * cw[0:1] + u * cw[1:2]
                + pltpu.roll(u, rows - 1, axis=0) * cw[2:3])
        conv = conv[HALO:HALO + ts]
        y_scr[:, c0:c0 + GROUP] = ((a_b * conv) * _silu(a_z)).astype(BF16)

    pos = i * ts + lax.broadcasted_iota(jnp.int32, (ts, 1), 0)

    def pool_dots(grp):
        c0 = 4 * w_conv + grp * GROUP
        return (_dot(hh, win_ref[:, c0:c0 + GROUP]), _dot(hm, win_ref[:, c0 + d:c0 + d + GROUP]))

    def pool_mix(grp, v, b_z):
        win = POOL_WINDOWS[grp]
        s = v + pltpu.roll(v, 1, axis=0)
        half = 1
        while 2 * half < win:
            s = pltpu.roll(s, half, axis=0) + pltpu.roll(s, rows - half, axis=0)
            half *= 2
        lo = jnp.maximum(pos - win // 2, 0)
        hi = jnp.minimum(pos + (win - win // 2), seq)
        inv_cnt = 1.0 / (hi - lo).astype(F32)
        dlt = s[HALO:HALO + ts] * inv_cnt - v[HALO:HALO + ts]
        yb = _dot(dlt.astype(BF16), pw_ref[grp])
        yb = yb * ps_ref[:, grp * GROUP:(grp + 1) * GROUP]
        y_scr[:, w_conv + grp * GROUP:w_conv + (grp + 1) * GROUP] = (yb * _silu(b_z)).astype(BF16)

    stages = []
    for blk in range(max(nblk, len(POOL_WINDOWS))):
        if blk < nblk:
            stages.append((conv_dots, conv_mix, blk))
        if blk < len(POOL_WINDOWS):
            stages.append((pool_dots, pool_mix, blk))
    ready = stages[0][0](stages[0][2])
    for k, (_, mix, blk) in enumerate(stages):
        cur = ready
        if k + 1 < len(stages):
            ready = stages[k + 1][0](stages[k + 1][2])
        mix(blk, *cur)

    out = x + _dot(y_scr[...], wout_ref[...])
    if final:
        out = _rms(out, fg_ref[...])
    o_ref[0] = out


def _even_layer(x, g, w_in, conv_w, pool_w, pool_scale, w_out, final_g, final):
    bsz, seq, d = x.shape
    ts = min(SEQ_TILE, seq)
    nt = seq // ts
    hb = ts // HALO
    nhb = seq // HALO
    kern = functools.partial(_even_kernel, ts=ts, seq=seq, d=d, final=final)
    return pl.pallas_call(
        kern,
        grid=(bsz, nt),
        in_specs=[
            pl.BlockSpec((1, HALO, d), lambda b, i: (b, jnp.maximum(i * hb - 1, 0), 0)),
            pl.BlockSpec((1, ts, d), lambda b, i: (b, i, 0)),
            pl.BlockSpec((1, HALO, d), lambda b, i: (b, jnp.minimum((i + 1) * hb, nhb - 1), 0)),
            _const_spec((1, d)),
            _const_spec(w_in.shape),
            _const_spec(conv_w.shape),
            _const_spec(pool_w.shape),
            _const_spec((1, d)),
            _const_spec(w_out.shape),
            _const_spec((1, d)),
        ],
        out_specs=pl.BlockSpec((1, ts, d), lambda b, i: (b, i, 0)),
        out_shape=jax.ShapeDtypeStruct(x.shape, F32),
        scratch_shapes=[pltpu.VMEM((ts + 2 * HALO, d), BF16),
                        pltpu.VMEM((ts, 2 * d), BF16)],
        compiler_params=_params(("parallel", "arbitrary")),
    )(x, x, x, g.reshape(1, d), w_in, conv_w, pool_w, pool_scale.reshape(1, d), w_out,
      final_g.reshape(1, d))


def _odd_pq_kernel(x_ref, g_ref, wf_ref, ab_ref, pw_ref, qw_ref, *, d):
    hm = _rms(x_ref[0], g_ref[...]).astype(BF16)
    nchunk = x_ref.shape[1] // FFT_N2
    for grp in range(d // GROUP):
        c0 = grp * GROUP
        f = _dot(hm, wf_ref[:, c0:c0 + GROUP])
        pq = _dot(f.astype(BF16), ab_ref[grp])
        pw = _words(pq[:, :GROUP])
        qw = _words(pq[:, GROUP:])
        hw = FFT_N2 // 2
        for j in range(nchunk):
            pw_ref[0, :, j, c0:c0 + GROUP] = pw[j * hw:(j + 1) * hw]
            qw_ref[0, :, j, c0:c0 + GROUP] = qw[j * hw:(j + 1) * hw]


def _odd_pq(x, g, w_f, ab):
    bsz, seq, d = x.shape
    n1 = seq // FFT_N2
    ts = min(PQ_TILE, seq)
    oblk = pl.BlockSpec((1, FFT_N2 // 2, ts // FFT_N2, d), lambda b, i: (b, 0, i, 0))
    kern = functools.partial(_odd_pq_kernel, d=d)
    wshape = jax.ShapeDtypeStruct((bsz, FFT_N2 // 2, n1, d), U32)
    return pl.pallas_call(
        kern,
        grid=(bsz, seq // ts),
        in_specs=[pl.BlockSpec((1, ts, d), lambda b, i: (b, i, 0)), _const_spec((1, d)),
                  _const_spec(w_f.shape), _const_spec(ab.shape)],
        out_specs=[oblk, oblk],
        out_shape=[wshape, wshape],
        compiler_params=_params(("parallel", "parallel")),
    )(x, g.reshape(1, d), w_f, ab)


def _fft_kernel(pw_ref, qw_ref, m1_ref, m2_ref, sw_ref, y_scr, *, n1):
    nw = FFT_N2 // 2
    for w2 in range(nw):
        rhs = jnp.concatenate([_rows(pw_ref[w2]), _rows(qw_ref[w2])], axis=0)
        y_scr[:, w2, :] = _words(_dot(m1_ref[w2], rhs))
    m2 = m2_ref[...]
    for m in range(n1 // 2):
        rhs = jnp.concatenate([_rows(y_scr[2 * m]), _rows(y_scr[n1 + 2 * m]),
                               _rows(y_scr[2 * m + 1]), _rows(y_scr[n1 + 2 * m + 1])], axis=0)
        sw_ref[:, m, :] = _words(_dot(m2, rhs))


def _fft(pw, qw, m1, m2):
    bsz, nw, n1, d = pw.shape
    lanes = FFT_LANES
    iblk = pl.BlockSpec((None, nw, n1, lanes), lambda b, c: (b, 0, 0, c))
    kern = functools.partial(_fft_kernel, n1=n1)
    sw = pl.pallas_call(
        kern,
        grid=(bsz, d // lanes),
        in_specs=[iblk, iblk, _const_spec(m1.shape), _const_spec(m2.shape)],
        out_specs=pl.BlockSpec((None, FFT_N2, n1 // 2, lanes), lambda b, c: (b, 0, 0, c)),
        out_shape=jax.ShapeDtypeStruct((bsz, FFT_N2, n1 // 2, d), U32),
        scratch_shapes=[pltpu.VMEM((2 * n1, nw, lanes), U32)],
        compiler_params=_params(("parallel", "parallel")),
    )(pw, qw, m1, m2)
    return sw.reshape(bsz, FFT_N2 * n1 // 2, d)


def _odd_main_kernel(x_ref, sw_ref, g_ref, win_ref, sg_ref, ws_ref, bs_ref, wout_ref, fg_ref,
                     o_ref, y_scr, *, ts, d, final):
    x = x_ref[0]
    hm = _rms(x, g_ref[...]).astype(BF16)
    nhead = d // GROUP

    def v_proj(hd):
        return _dot(hm, win_ref[:, d + hd * GROUP:d + (hd + 1) * GROUP])

    v_next = v_proj(0)
    for hd in range(nhead):
        c0 = hd * GROUP
        v = v_next
        c_u = _dot(hm, win_ref[:, c0:c0 + GROUP])
        c_z = _dot(hm, win_ref[:, 2 * d + c0:2 * d + c0 + GROUP])
        d_z = _dot(hm, win_ref[:, 3 * d + c0:3 * d + c0 + GROUP])
        if hd + 1 < nhead:
            v_next = v_proj(hd + 1)
        mu = jnp.mean(v, axis=-1, keepdims=True)
        vc = v - mu
        var = jnp.mean(vc * vc, axis=-1, keepdims=True)
        vn = ((vc * lax.rsqrt(var + EPS)) * sg_ref[:, c0:c0 + GROUP]).astype(BF16)
        ws = ws_ref[hd]
        mix = jnp.concatenate(
            [_dot(ws, vn[c * CHUNK:(c + 1) * CHUNK]) + bs_ref[hd] for c in range(ts // CHUNK)],
            axis=0)
        y_scr[:, c0:c0 + GROUP] = ((c_u * mix) * _silu(c_z)).astype(BF16)
        spec = _rows(sw_ref[0, :, c0:c0 + GROUP]).astype(F32)
        y_scr[:, d + c0:d + c0 + GROUP] = (spec * _silu(d_z)).astype(BF16)
    out = x + _dot(y_scr[...], wout_ref[...])
    if final:
        out = _rms(out, fg_ref[...])
    o_ref[0] = out


def _odd_main(x, sw, g, w_main, sgu_g, sgu_ws, sgu_bs, w_out, final_g, final):
    bsz, seq, d = x.shape
    ts = min(SEQ_TILE, seq)
    tile = pl.BlockSpec((1, ts, d), lambda b, i: (b, i, 0))
    kern = functools.partial(_odd_main_kernel, ts=ts, d=d, final=final)
    return pl.pallas_call(
        kern,
        grid=(bsz, seq // ts),
        in_specs=[tile, pl.BlockSpec((1, ts // 2, d), lambda b, i: (b, i, 0)), _const_spec((1, d)),
                  _const_spec(w_main.shape), _const_spec((1, d)), _const_spec(sgu_ws.shape),
                  _const_spec(sgu_bs.shape), _const_spec(w_out.shape), _const_spec((1, d))],
        out_specs=tile,
        out_shape=jax.ShapeDtypeStruct(x.shape, F32),
        scratch_shapes=[pltpu.VMEM((ts, 2 * d), BF16)],
        compiler_params=_params(("parallel", "parallel")),
    )(x, sw, g.reshape(1, d), w_main, sgu_g.reshape(1, d), sgu_ws, sgu_bs, w_out,
      final_g.reshape(1, d))


def _fold_kernel(c_ref, s_ref, w_ref, ab_ref):
    w = w_ref[0]
    a = jnp.dot(c_ref[...], w, preferred_element_type=F32, precision=lax.Precision.HIGHEST)
    b = jnp.dot(s_ref[...], w, preferred_element_type=F32, precision=lax.Precision.HIGHEST)
    ab_ref[0, :, :GROUP] = a.astype(BF16)
    ab_ref[0, :, GROUP:] = b.astype(BF16)


def _fold_fnet(fnet_w):
    ng, dh, _ = fnet_w.shape
    ang = 2.0 * np.pi * np.outer(np.arange(dh), np.arange(dh)) / dh
    scale = dh ** -0.5
    cc = jnp.asarray(np.cos(ang) * scale, F32)
    sc = jnp.asarray(np.sin(ang) * scale, F32)
    return pl.pallas_call(
        _fold_kernel,
        grid=(ng,),
        in_specs=[pl.BlockSpec((dh, dh), lambda g: (0, 0)), pl.BlockSpec((dh, dh), lambda g: (0, 0)),
                  pl.BlockSpec((1, dh, dh), lambda g: (g, 0, 0))],
        out_specs=pl.BlockSpec((1, dh, 2 * dh), lambda g: (g, 0, 0)),
        out_shape=jax.ShapeDtypeStruct((ng, dh, 2 * dh), BF16),
    )(cc, sc, fnet_w)


def _round_kernel(t_ref, o_ref):
    o_ref[...] = t_ref[...].astype(BF16)


def _round_bf16(table):
    lead = table.shape[0]
    per = max(1, ROUND_BLOCK_BYTES // (4 * int(np.prod(table.shape[1:]))))
    nb = min(lead, per)
    while lead % nb:
        nb -= 1
    blk = pl.BlockSpec((nb,) + table.shape[1:], lambda i: (i,) + (0,) * (table.ndim - 1))
    return pl.pallas_call(_round_kernel, grid=(lead // nb,), in_specs=[blk], out_specs=blk,
                          out_shape=jax.ShapeDtypeStruct(table.shape, BF16))(table)


def _fft_tables(seq, n1):
    n2 = seq // n1
    nw = n2 // 2
    w2 = np.arange(nw)[:, None, None, None]
    e = np.arange(2)[None, None, :, None]
    k1 = np.arange(n1)[None, :, None, None]
    t1 = np.arange(n1)[None, None, None, :]
    ang = 2.0 * np.pi * ((k1 * (t1 * n2 + 2 * w2 + e)) % seq) / seq
    c = np.cos(ang) * n1 ** -0.5
    s = np.sin(ang) * n1 ** -0.5
    m1 = np.zeros((nw, 2, n1, 2, 2, n1, 2))
    for ee in range(2):
        m1[:, 0, :, ee, 0, :, ee] = c[:, :, ee, :]
        m1[:, 0, :, ee, 1, :, ee] = -s[:, :, ee, :]
        m1[:, 1, :, ee, 0, :, ee] = -s[:, :, ee, :]
        m1[:, 1, :, ee, 1, :, ee] = -c[:, :, ee, :]
    m1 = m1.reshape(nw, 4 * n1, 4 * n1)
    ang2 = 2.0 * np.pi * (np.outer(np.arange(n2), np.arange(n2)) % n2) / n2
    m2 = np.zeros((n2, 2, 2, 2, n2))
    for ee in range(2):
        m2[:, ee, ee, 0, :] = np.cos(ang2) * n2 ** -0.5
        m2[:, ee, ee, 1, :] = np.sin(ang2) * n2 ** -0.5
    m2 = m2.reshape(2 * n2, 4 * n2)
    return _round_bf16(jnp.asarray(m1, F32)), _round_bf16(jnp.asarray(m2, F32)[None])[0]


def _odd_layer(x, g, w_in, sgu_g, sgu_ws, sgu_bs, ab, w_out, final_g, final):
    bsz, seq, d = x.shape
    n1 = seq // FFT_N2
    m1, m2 = _fft_tables(seq, n1)
    w_f = w_in[:, 3 * d:4 * d]
    w_main = jnp.concatenate([w_in[:, :3 * d], w_in[:, 4 * d:]], axis=1)
    pw, qw = _odd_pq(x, g, w_f, ab)
    sw = _fft(pw, qw, m1, m2)
    return _odd_main(x, sw, g, w_main, sgu_g, sgu_ws, sgu_bs, w_out, final_g, final)


def _trunk(x, norm_g, ev, od, final_g):
    depth = norm_g.shape[0]
    for i in range(depth):
        j = i // 2
        final = i == depth - 1
        if i % 2 == 0:
            w_in, conv_w, pool_w, pool_scale, w_out = (t[j] for t in ev)
            x = _even_layer(x, norm_g[i], w_in, conv_w, pool_w, pool_scale, w_out, final_g, final)
        else:
            w_in, sgu_g, sgu_ws, sgu_bs, ab, w_out = (t[j] for t in od)
            x = _odd_layer(x, norm_g[i], w_in, sgu_g, sgu_ws, sgu_bs, ab, w_out, final_g, final)
    return x


def kernel(x_prompt, x_sample, norm_g, ev_w_in, ev_conv_w, ev_pool_w, ev_pool_scale, ev_w_out,
           od_w_in, od_sgu_norm_g, od_sgu_ws, od_sgu_bs, od_fnet_w, od_w_out, final_g):
    ev = (ev_w_in.astype(BF16), ev_conv_w, ev_pool_w.astype(BF16), ev_pool_scale,
          ev_w_out.astype(BF16))
    ab = jnp.stack([_fold_fnet(od_fnet_w[j]) for j in range(od_fnet_w.shape[0])])
    od = (od_w_in.astype(BF16), od_sgu_norm_g, od_sgu_ws.astype(BF16),
          od_sgu_bs[..., None], ab, od_w_out.astype(BF16))
    y_prompt = _trunk(x_prompt, norm_g, ev, od, final_g)
    y_sample = _trunk(x_sample, norm_g, ev, od, final_g)
    return (y_prompt, y_sample)
```

```python
import functools

import numpy as np
import jax
import jax.numpy as jnp
from jax import lax
from jax.experimental import pallas as pl
from jax.experimental.pallas import tpu as pltpu

EPS = 1e-6
CONV_K = 3
POOL_WINDOWS = (2, 4, 8, 16)
GROUP = 256
CHUNK = 128
FFT_N2 = 128
FFT_LANES = 256
LANES = 128

V7X_VMEM_BYTES = 64 * 1024 * 1024
BF16_SUBLANES = 16
WORD_SUBLANES = 8
MXU_TILE = 256
MXU_SPLIT_MIN_ROWS = 2 * MXU_TILE

HALO = BF16_SUBLANES
SEQ_TILE = 512
PQ_TILE = WORD_SUBLANES * FFT_N2
VMEM_LIMIT = V7X_VMEM_BYTES - 8 * 1024 * 1024
ROUND_BLOCK_BYTES = 2 * 1024 * 1024

BF16 = jnp.bfloat16
F32 = jnp.float32
U32 = jnp.uint32


def _dot(a, b):
    m = a.shape[0]
    if m >= MXU_SPLIT_MIN_ROWS and m % (2 * BF16_SUBLANES) == 0:
        h = m // 2
        return jnp.concatenate([jnp.dot(a[:h], b, preferred_element_type=F32),
                                jnp.dot(a[h:], b, preferred_element_type=F32)], axis=0)
    return jnp.dot(a, b, preferred_element_type=F32)


def _rms(x, g):
    ms = jnp.mean(x * x, axis=-1, keepdims=True)
    return (x * lax.rsqrt(ms + EPS)) * g


def _silu(z):
    return z * (1.0 / (1.0 + jnp.exp(-z)))


def _words(x):
    return pltpu.bitcast(x.astype(BF16), U32)


def _rows(w):
    return pltpu.bitcast(w, BF16)


def _store_rows(ref, j, pitch, val):
    ref[pl.ds(j, val.shape[0], stride=pitch), :] = val


def _const_spec(shape):
    nd = len(shape)
    return pl.BlockSpec(shape, lambda *_: (0,) * nd, pipeline_mode=pl.Buffered(1))


def _params(sem):
    return pltpu.CompilerParams(dimension_semantics=sem, vmem_limit_bytes=VMEM_LIMIT)


def _even_kernel(xp_ref, x_ref, xn_ref, g_ref, win_ref, cw_ref, pw_ref, ps_ref, wout_ref,
                 fg_ref, o_ref, h_scr, y_scr, *, ts, seq, d, final):
    i = pl.program_id(1)
    n = pl.num_programs(1)
    g = g_ref[...]
    x = x_ref[0]
    hp = _rms(xp_ref[0], g)
    hn = _rms(xn_ref[0], g)
    h_scr[0:HALO, :] = jnp.where(i > 0, hp, 0.0).astype(BF16)
    h_scr[HALO:HALO + ts, :] = _rms(x, g).astype(BF16)
    h_scr[HALO + ts:, :] = jnp.where(i < n - 1, hn, 0.0).astype(BF16)

    rows = ts + 2 * HALO
    hh = h_scr[...]
    hm = h_scr[HALO:HALO + ts, :]
    w_conv = d
    nblk = d // GROUP

    def conv_dots(cb):
        c0 = cb * GROUP
        return (_dot(hh, win_ref[:, c0:c0 + GROUP]),
                _dot(hh, win_ref[:, 2 * w_conv + c0:2 * w_conv + c0 + GROUP]),
                _dot(hm, win_ref[:, w_conv + c0:w_conv + c0 + GROUP]),
                _dot(hm, win_ref[:, 3 * w_conv + c0:3 * w_conv + c0 + GROUP]))

    def conv_mix(cb, a_h, a_c, a_b, a_z):
        c0 = cb * GROUP
        u = a_c * a_h
        cw = cw_ref[:, c0:c0 + GROUP]
        conv = (pltpu.roll(u, 1, axis=0) * cw[0:1] + u * cw[1:2]
                + pltpu.roll(u, rows - 1, axis=0) * cw[2:3])
        conv = conv[HALO:HALO + ts]
        y_scr[:, c0:c0 + GROUP] = ((a_b * conv) * _silu(a_z)).astype(BF16)

    pos = i * ts + lax.broadcasted_iota(jnp.int32, (ts, 1), 0)

    def pool_dots(grp):
        c0 = 4 * w_conv + grp * GROUP
        return (_dot(hh, win_ref[:, c0:c0 + GROUP]), _dot(hm, win_ref[:, c0 + d:c0 + d + GROUP]))

    def pool_mix(grp, v, b_z):
        win = POOL_WINDOWS[grp]
        s = v + pltpu.roll(v, 1, axis=0)
        half = 1
        while 2 * half < win:
            s = pltpu.roll(s, half, axis=0) + pltpu.roll(s, rows - half, axis=0)
            half *= 2
        lo = jnp.maximum(pos - win // 2, 0)
        hi = jnp.minimum(pos + (win - win // 2), seq)
        inv_cnt = 1.0 / (hi - lo).astype(F32)
        dlt = s[HALO:HALO + ts] * inv_cnt - v[HALO:HALO + ts]
        yb = _dot(dlt.astype(BF16), pw_ref[grp])
        yb = yb * ps_ref[:, grp * GROUP:(grp + 1) * GROUP]
        y_scr[:, w_conv + grp * GROUP:w_conv + (grp + 1) * GROUP] = (yb * _silu(b_z)).astype(BF16)

    stages = []
    for blk in range(max(nblk, len(POOL_WINDOWS))):
        if blk < nblk:
            stages.append((conv_dots, conv_mix, blk))
        if blk < len(POOL_WINDOWS):
            stages.append((pool_dots, pool_mix, blk))
    ready = stages[0][0](stages[0][2])
    for k, (_, mix, blk) in enumerate(stages):
        cur = ready
        if k + 1 < len(stages):
            ready = stages[k + 1][0](stages[k + 1][2])
        mix(blk, *cur)

    out = x + _dot(y_scr[...], wout_ref[...])
    if final:
        out = _rms(out, fg_ref[...])
    o_ref[0] = out


def _even_layer(x, g, w_in, conv_w, pool_w, pool_scale, w_out, final_g, final):
    bsz, seq, d = x.shape
    ts = min(SEQ_TILE, seq)
    nt = seq // ts
    hb = ts // HALO
    nhb = seq // HALO
    kern = functools.partial(_even_kernel, ts=ts, seq=seq, d=d, final=final)
    return pl.pallas_call(
        kern,
        grid=(bsz, nt),
        in_specs=[
            pl.BlockSpec((1, HALO, d), lambda b, i: (b, jnp.maximum(i * hb - 1, 0), 0)),
            pl.BlockSpec((1, ts, d), lambda b, i: (b, i, 0)),
            pl.BlockSpec((1, HALO, d), lambda b, i: (b, jnp.minimum((i + 1) * hb, nhb - 1), 0)),
            _const_spec((1, d)),
            _const_spec(w_in.shape),
            _const_spec(conv_w.shape),
            _const_spec(pool_w.shape),
            _const_spec((1, d)),
            _const_spec(w_out.shape),
            _const_spec((1, d)),
        ],
        out_specs=pl.BlockSpec((1, ts, d), lambda b, i: (b, i, 0)),
        out_shape=jax.ShapeDtypeStruct(x.shape, F32),
        scratch_shapes=[pltpu.VMEM((ts + 2 * HALO, d), BF16),
                        pltpu.VMEM((ts, 2 * d), BF16)],
        compiler_params=_params(("parallel", "arbitrary")),
    )(x, x, x, g.reshape(1, d), w_in, conv_w, pool_w, pool_scale.reshape(1, d), w_out,
      final_g.reshape(1, d))


def _odd_pq_kernel(x_ref, g_ref, wf_ref, ab_ref, pw_ref, qw_ref, *, d):
    hm = _rms(x_ref[0], g_ref[...]).astype(BF16)
    nchunk = x_ref.shape[1] // FFT_N2
    for grp in range(d // GROUP):
        c0 = grp * GROUP
        f = _dot(hm, wf_ref[:, c0:c0 + GROUP])
        pq = _dot(f.astype(BF16), ab_ref[grp])
        pw = _words(pq[:, :GROUP])
        qw = _words(pq[:, GROUP:])
        hw = FFT_N2 // 2
        for j in range(nchunk):
            for l in range(GROUP // LANES):
                sl = grp * (GROUP // LANES) + l
                _store_rows(pw_ref.at[0, sl, 0], j, nchunk,
                            pw[j * hw:(j + 1) * hw, l * LANES:(l + 1) * LANES])
                _store_rows(qw_ref.at[0, sl, 0], j, nchunk,
                            qw[j * hw:(j + 1) * hw, l * LANES:(l + 1) * LANES])


def _odd_pq(x, g, w_f, ab):
    bsz, seq, d = x.shape
    ts = PQ_TILE
    nt = seq // ts
    rows = FFT_N2 // 2 * WORD_SUBLANES
    oblk = pl.BlockSpec((1, d // LANES, 1, rows, LANES), lambda b, i: (b, 0, i, 0, 0))
    kern = functools.partial(_odd_pq_kernel, d=d)
    wshape = jax.ShapeDtypeStruct((bsz, d // LANES, nt, rows, LANES), U32)
    return pl.pallas_call(
        kern,
        grid=(bsz, seq // ts),
        in_specs=[pl.BlockSpec((1, ts, d), lambda b, i: (b, i, 0)), _const_spec((1, d)),
                  _const_spec(w_f.shape), _const_spec(ab.shape)],
        out_specs=[oblk, oblk],
        out_shape=[wshape, wshape],
        compiler_params=_params(("parallel", "parallel")),
    )(x, g.reshape(1, d), w_f, ab)


def _odd_tile_rows(n):
    tiles = -(-n // WORD_SUBLANES)
    return WORD_SUBLANES * (tiles if tiles % 2 else tiles + 1)


def _fft_kernel(pw_ref, qw_ref, m1_ref, m2_ref, sw_ref, y_scr, *, n1, mp, nwp):
    nw = FFT_N2 // 2
    nslab = pw_ref.shape[0]
    ws = WORD_SUBLANES

    def slabs(get):
        return jnp.concatenate([get(s) for s in range(nslab)], axis=1)

    def t1_rows(ref, s, w2):
        return jnp.concatenate([ref[s, i, w2 * ws:(w2 + 1) * ws, :] for i in range(n1 // ws)], axis=0)

    for w2 in range(nw):
        rhs = jnp.concatenate([slabs(lambda s: _rows(t1_rows(pw_ref, s, w2))),
                               slabs(lambda s: _rows(t1_rows(qw_ref, s, w2)))], axis=0)
        wds = _words(_dot(m1_ref[w2], rhs))
        for s in range(nslab):
            _store_rows(y_scr.at[s], w2, nwp, wds[:, s * LANES:(s + 1) * LANES])
    m2 = m2_ref[...]
    for m in range(n1 // 2):
        rhs = jnp.concatenate([slabs(lambda s: _rows(y_scr[s, k * nwp:k * nwp + nw, :]))
                               for k in (2 * m, n1 + 2 * m, 2 * m + 1, n1 + 2 * m + 1)], axis=0)
        wds = _words(_dot(m2, rhs))
        for s in range(nslab):
            _store_rows(sw_ref.at[s], m, mp, wds[:, s * LANES:(s + 1) * LANES])
    if mp > n1 // 2:
        zeros = _words(jnp.zeros((2 * (mp - n1 // 2), LANES), F32))
        for s in range(nslab):
            for k2 in range(FFT_N2):
                sw_ref[s, k2 * mp + n1 // 2:(k2 + 1) * mp, :] = zeros


def _fft(pw, qw, m1, m2, n1):
    bsz, nsl, nt, rows, _ = pw.shape
    nw = FFT_N2 // 2
    per = FFT_LANES // LANES
    mp = _odd_tile_rows(n1 // 2)
    nwp = _odd_tile_rows(nw)
    iblk = pl.BlockSpec((None, per, nt, rows, LANES), lambda b, c: (b, c, 0, 0, 0))
    kern = functools.partial(_fft_kernel, n1=n1, mp=mp, nwp=nwp)
    return pl.pallas_call(
        kern,
        grid=(bsz, nsl // per),
        in_specs=[iblk, iblk, _const_spec(m1.shape), _const_spec(m2.shape)],
        out_specs=pl.BlockSpec((None, per, FFT_N2 * mp, LANES), lambda b, c: (b, c, 0, 0)),
        out_shape=jax.ShapeDtypeStruct((bsz, nsl, FFT_N2 * mp, LANES), U32),
        scratch_shapes=[pltpu.VMEM((per, 2 * n1 * nwp, LANES), U32)],
        compiler_params=_params(("parallel", "parallel")),
    )(pw, qw, m1, m2)


def _odd_main_kernel(x_ref, sw_ref, g_ref, win_ref, sg_ref, ws_ref, bs_ref, wout_ref, fg_ref,
                     o_ref, y_scr, *, ts, d, half_n1, mp, final):
    x = x_ref[0]
    hm = _rms(x, g_ref[...]).astype(BF16)
    nhead = d // GROUP

    def v_proj(hd):
        return _dot(hm, win_ref[:, d + hd * GROUP:d + (hd + 1) * GROUP])

    v_next = v_proj(0)
    for hd in range(nhead):
        c0 = hd * GROUP
        v = v_next
        c_u = _dot(hm, win_ref[:, c0:c0 + GROUP])
        c_z = _dot(hm, win_ref[:, 2 * d + c0:2 * d + c0 + GROUP])
        d_z = _dot(hm, win_ref[:, 3 * d + c0:3 * d + c0 + GROUP])
        if hd + 1 < nhead:
            v_next = v_proj(hd + 1)
        mu = jnp.mean(v, axis=-1, keepdims=True)
        vc = v - mu
        var = jnp.mean(vc * vc, axis=-1, keepdims=True)
        vn = ((vc * lax.rsqrt(var + EPS)) * sg_ref[:, c0:c0 + GROUP]).astype(BF16)
        ws = ws_ref[hd]
        mix = jnp.concatenate(
            [_dot(ws, vn[c * CHUNK:(c + 1) * CHUNK]) + bs_ref[hd] for c in range(ts // CHUNK)],
            axis=0)
        y_scr[:, c0:c0 + GROUP] = ((c_u * mix) * _silu(c_z)).astype(BF16)
        per = GROUP // LANES
        spec = jnp.concatenate(
            [_rows(sw_ref[0, hd * per + l].reshape(-1, mp, LANES)[:, :half_n1].reshape(ts // 2, LANES))
             for l in range(per)], axis=1).astype(F32)
        y_scr[:, d + c0:d + c0 + GROUP] = (spec * _silu(d_z)).astype(BF16)
    out = x + _dot(y_scr[...], wout_ref[...])
    if final:
        out = _rms(out, fg_ref[...])
    o_ref[0] = out


def _odd_main(x, sw, g, w_main, sgu_g, sgu_ws, sgu_bs, w_out, final_g, final):
    bsz, seq, d = x.shape
    ts = min(SEQ_TILE, seq)
    half_n1 = seq // FFT_N2 // 2
    nk2 = ts // 2 // half_n1
    tile = pl.BlockSpec((1, ts, d), lambda b, i: (b, i, 0))
    mp = sw.shape[2] // FFT_N2
    swblk = pl.BlockSpec((1, sw.shape[1], nk2 * mp, LANES), lambda b, i: (b, 0, i, 0))
    kern = functools.partial(_odd_main_kernel, ts=ts, d=d, half_n1=half_n1, mp=mp, final=final)
    return pl.pallas_call(
        kern,
        grid=(bsz, seq // ts),
        in_specs=[tile, swblk, _const_spec((1, d)),
                  _const_spec(w_main.shape), _const_spec((1, d)), _const_spec(sgu_ws.shape),
                  _const_spec(sgu_bs.shape), _const_spec(w_out.shape), _const_spec((1, d))],
        out_specs=tile,
        out_shape=jax.ShapeDtypeStruct(x.shape, F32),
        scratch_shapes=[pltpu.VMEM((ts, 2 * d), BF16)],
        compiler_params=_params(("parallel", "parallel")),
    )(x, sw, g.reshape(1, d), w_main, sgu_g.reshape(1, d), sgu_ws, sgu_bs, w_out,
      final_g.reshape(1, d))


def _fold_kernel(c_ref, s_ref, w_ref, ab_ref):
    w = w_ref[0]
    a = jnp.dot(c_ref[...], w, preferred_element_type=F32, precision=lax.Precision.HIGHEST)
    b = jnp.dot(s_ref[...], w, preferred_element_type=F32, precision=lax.Precision.HIGHEST)
    ab_ref[0, :, :GROUP] = a.astype(BF16)
    ab_ref[0, :, GROUP:] = b.astype(BF16)


def _fold_fnet(fnet_w):
    ng, dh, _ = fnet_w.shape
    ang = 2.0 * np.pi * np.outer(np.arange(dh), np.arange(dh)) / dh
    scale = dh ** -0.5
    cc = jnp.asarray(np.cos(ang) * scale, F32)
    sc = jnp.asarray(np.sin(ang) * scale, F32)
    return pl.pallas_call(
        _fold_kernel,
        grid=(ng,),
        in_specs=[pl.BlockSpec((dh, dh), lambda g: (0, 0)), pl.BlockSpec((dh, dh), lambda g: (0, 0)),
                  pl.BlockSpec((1, dh, dh), lambda g: (g, 0, 0))],
        out_specs=pl.BlockSpec((1, dh, 2 * dh), lambda g: (g, 0, 0)),
        out_shape=jax.ShapeDtypeStruct((ng, dh, 2 * dh), BF16),
    )(cc, sc, fnet_w)


def _round_kernel(t_ref, o_ref):
    o_ref[...] = t_ref[...].astype(BF16)


def _round_bf16(table):
    lead = table.shape[0]
    per = max(1, ROUND_BLOCK_BYTES // (4 * int(np.prod(table.shape[1:]))))
    nb = min(lead, per)
    while lead % nb:
        nb -= 1
    blk = pl.BlockSpec((nb,) + table.shape[1:], lambda i: (i,) + (0,) * (table.ndim - 1))
    return pl.pallas_call(_round_kernel, grid=(lead // nb,), in_specs=[blk], out_specs=blk,
                          out_shape=jax.ShapeDtypeStruct(table.shape, BF16))(table)


def _fft_tables(seq, n1):
    n2 = seq // n1
    nw = n2 // 2
    w2 = np.arange(nw)[:, None, None, None]
    e = np.arange(2)[None, None, :, None]
    k1 = np.arange(n1)[None, :, None, None]
    t1 = np.arange(n1)[None, None, None, :]
    ang = 2.0 * np.pi * ((k1 * (t1 * n2 + 2 * w2 + e)) % seq) / seq
    c = np.cos(ang) * n1 ** -0.5
    s = np.sin(ang) * n1 ** -0.5
    m1 = np.zeros((nw, 2, n1, 2, 2, n1, 2))
    for ee in range(2):
        m1[:, 0, :, ee, 0, :, ee] = c[:, :, ee, :]
        m1[:, 0, :, ee, 1, :, ee] = -s[:, :, ee, :]
        m1[:, 1, :, ee, 0, :, ee] = -s[:, :, ee, :]
        m1[:, 1, :, ee, 1, :, ee] = -c[:, :, ee, :]
    m1 = m1.reshape(nw, 4 * n1, 4 * n1)
    ang2 = 2.0 * np.pi * (np.outer(np.arange(n2), np.arange(n2)) % n2) / n2
    m2 = np.zeros((n2, 2, 2, 2, n2))
    for ee in range(2):
        m2[:, ee, ee, 0, :] = np.cos(ang2) * n2 ** -0.5
        m2[:, ee, ee, 1, :] = np.sin(ang2) * n2 ** -0.5
    m2 = m2.reshape(2 * n2, 4 * n2)
    return _round_bf16(jnp.asarray(m1, F32)), _round_bf16(jnp.asarray(m2, F32)[None])[0]


def _odd_layer(x, g, w_in, sgu_g, sgu_ws, sgu_bs, ab, w_out, final_g, final):
    bsz, seq, d = x.shape
    n1 = seq // FFT_N2
    m1, m2 = _fft_tables(seq, n1)
    w_f = w_in[:, 3 * d:4 * d]
    w_main = jnp.concatenate([w_in[:, :3 * d], w_in[:, 4 * d:]], axis=1)
    pw, qw = _odd_pq(x, g, w_f, ab)
    sw = _fft(pw, qw, m1, m2, n1)
    return _odd_main(x, sw, g, w_main, sgu_g, sgu_ws, sgu_bs, w_out, final_g, final)


def _trunk(x, norm_g, ev, od, final_g):
    depth = norm_g.shape[0]
    for i in range(depth):
        j = i // 2
        final = i == depth - 1
        if i % 2 == 0:
            w_in, conv_w, pool_w, pool_scale, w_out = (t[j] for t in ev)
            x = _even_layer(x, norm_g[i], w_in, conv_w, pool_w, pool_scale, w_out, final_g, final)
        else:
            w_in, sgu_g, sgu_ws, sgu_bs, ab, w_out = (t[j] for t in od)
            x = _odd_layer(x, norm_g[i], w_in, sgu_g, sgu_ws, sgu_bs, ab, w_out, final_g, final)
    return x


def kernel(x_prompt, x_sample, norm_g, ev_w_in, ev_conv_w, ev_pool_w, ev_pool_scale, ev_w_out,
           od_w_in, od_sgu_norm_g, od_sgu_ws, od_sgu_bs, od_fnet_w, od_w_out, final_g):
    ev = (ev_w_in.astype(BF16), ev_conv_w, ev_pool_w.astype(BF16), ev_pool_scale,
          ev_w_out.astype(BF16))
    ab = jnp.stack([_fold_fnet(od_fnet_w[j]) for j in range(od_fnet_w.shape[0])])
    od = (od_w_in.astype(BF16), od_sgu_norm_g, od_sgu_ws.astype(BF16),
          od_sgu_bs[..., None], ab, od_w_out.astype(BF16))
    y_prompt = _trunk(x_prompt, norm_g, ev, od, final_g)
    y_sample = _trunk(x_sample, norm_g, ev, od, final_g)
    return (y_prompt, y_sample)
```

```python
import functools

import numpy as np
import jax
import jax.numpy as jnp
from jax import lax
from jax.experimental import pallas as pl
from jax.experimental.pallas import tpu as pltpu

EPS = 1e-6
CONV_K = 3
POOL_WINDOWS = (2, 4, 8, 16)
GROUP = 256
CHUNK = 128
FFT_N2 = 128
FFT_LANES = 256
LANES = 128

V7X_VMEM_BYTES = 64 * 1024 * 1024
BF16_SUBLANES = 16
WORD_SUBLANES = 8
MXU_TILE = 256
MXU_SPLIT_MIN_ROWS = 2 * MXU_TILE

HALO = BF16_SUBLANES
SEQ_TILE = 512
EVEN_TILE = 1024
EVEN_LAG = 3
MAIN_TILE = 1024
MAIN_LAG = 1
PQ_TILE = WORD_SUBLANES * FFT_N2
VMEM_LIMIT = V7X_VMEM_BYTES - 8 * 1024 * 1024
ROUND_BLOCK_BYTES = 2 * 1024 * 1024

BF16 = jnp.bfloat16
F32 = jnp.float32
U32 = jnp.uint32


def _dot(a, b):
    m = a.shape[0]
    if m >= MXU_SPLIT_MIN_ROWS and m % (2 * BF16_SUBLANES) == 0:
        h = m // 2
        return jnp.concatenate([jnp.dot(a[:h], b, preferred_element_type=F32),
                                jnp.dot(a[h:], b, preferred_element_type=F32)], axis=0)
    return jnp.dot(a, b, preferred_element_type=F32)


def _rms(x, g):
    ms = jnp.mean(x * x, axis=-1, keepdims=True)
    return (x * lax.rsqrt(ms + EPS)) * g


def _silu(z):
    hz = 0.5 * z
    return hz * jnp.tanh(hz) + hz


def _words(x):
    return pltpu.bitcast(x.astype(BF16), U32)


def _rows(w):
    return pltpu.bitcast(w, BF16)


def _store_rows(ref, j, pitch, val):
    ref[pl.ds(j, val.shape[0], stride=pitch), :] = val


def _interleave(streams, lag):
    live = [True] * len(streams)
    step = 0
    while any(live):
        for k, gen in enumerate(streams):
            if live[k] and step >= k * lag:
                try:
                    next(gen)
                except StopIteration:
                    live[k] = False
        step += 1


def _const_spec(shape):
    nd = len(shape)
    return pl.BlockSpec(shape, lambda *_: (0,) * nd, pipeline_mode=pl.Buffered(1))


def _params(sem):
    return pltpu.CompilerParams(dimension_semantics=sem, vmem_limit_bytes=VMEM_LIMIT)


def _even_kernel(xp_ref, x_ref, xn_ref, g_ref, win_ref, cw_ref, pw_ref, ps_ref, wout_ref,
                 fg_ref, o_ref, *, ts, sub, lag, seq, d, final):
    i = pl.program_id(1)
    n = pl.num_programs(1)
    g = g_ref[...]
    w_conv = d
    nblk = d // GROUP
    rows = sub + 2 * HALO
    nsub = ts // sub
    hp = jnp.where(i > 0, _rms(xp_ref[0], g), 0.0).astype(BF16)
    hn = jnp.where(i < n - 1, _rms(xn_ref[0], g), 0.0).astype(BF16)

    def halo(r0):
        return _rms(x_ref[0, r0:r0 + HALO, :], g).astype(BF16)

    def stream(k):
        r0 = k * sub
        x = x_ref[0, r0:r0 + sub, :]
        hm = _rms(x, g).astype(BF16)
        hh = jnp.concatenate([hp if k == 0 else halo(r0 - HALO), hm,
                              hn if k == nsub - 1 else halo(r0 + sub)], axis=0)
        pos = i * ts + r0 + lax.broadcasted_iota(jnp.int32, (sub, 1), 0)
        ys = [None] * (nblk + len(POOL_WINDOWS))

        def conv_dots(cb):
            c0 = cb * GROUP
            return (_dot(hh, win_ref[:, c0:c0 + GROUP]),
                    _dot(hh, win_ref[:, 2 * w_conv + c0:2 * w_conv + c0 + GROUP]),
                    _dot(hm, win_ref[:, w_conv + c0:w_conv + c0 + GROUP]),
                    _dot(hm, win_ref[:, 3 * w_conv + c0:3 * w_conv + c0 + GROUP]))

        def conv_mix(cb, a_h, a_c, a_b, a_z):
            c0 = cb * GROUP
            u = a_c * a_h
            cw = cw_ref[:, c0:c0 + GROUP]
            conv = (pltpu.roll(u, 1, axis=0) * cw[0:1] + u * cw[1:2]
                    + pltpu.roll(u, rows - 1, axis=0) * cw[2:3])
            conv = conv[HALO:HALO + sub]
            ys[cb] = ((a_b * conv) * _silu(a_z)).astype(BF16)

        def pool_dots(grp):
            c0 = 4 * w_conv + grp * GROUP
            return (_dot(hh, win_ref[:, c0:c0 + GROUP]), _dot(hm, win_ref[:, c0 + d:c0 + d + GROUP]))

        def pool_mix(grp, v, b_z):
            win = POOL_WINDOWS[grp]
            q, span = v, 1
            while 2 * span < win:
                q = q + pltpu.roll(q, rows - span, axis=0)
                span *= 2
            s = pltpu.roll(q, span, axis=0) + q
            lo = jnp.maximum(pos - win // 2, 0)
            hi = jnp.minimum(pos + (win - win // 2), seq)
            inv_cnt = 1.0 / (hi - lo).astype(F32)
            dlt = s[HALO:HALO + sub] * inv_cnt - v[HALO:HALO + sub]
            yb = _dot(dlt.astype(BF16), pw_ref[grp])
            yb = yb * ps_ref[:, grp * GROUP:(grp + 1) * GROUP]
            ys[nblk + grp] = (yb * _silu(b_z)).astype(BF16)

        stages = []
        for blk in range(max(nblk, len(POOL_WINDOWS))):
            if blk < nblk:
                stages.append((conv_dots, conv_mix, blk))
            if blk < len(POOL_WINDOWS):
                stages.append((pool_dots, pool_mix, blk))
        ready = stages[0][0](stages[0][2])
        yield
        for j, (_, mix, blk) in enumerate(stages):
            cur = ready
            if j + 1 < len(stages):
                ready = stages[j + 1][0](stages[j + 1][2])
            mix(blk, *cur)
            yield
        out = x + _dot(jnp.concatenate(ys, axis=1), wout_ref[...])
        if final:
            out = _rms(out, fg_ref[...])
        o_ref[0, r0:r0 + sub, :] = out
        yield

    _interleave([stream(k) for k in range(nsub)], lag)


def _even_layer(x, g, w_in, conv_w, pool_w, pool_scale, w_out, final_g, final):
    bsz, seq, d = x.shape
    ts = min(EVEN_TILE, seq)
    nt = seq // ts
    hb = ts // HALO
    nhb = seq // HALO
    kern = functools.partial(_even_kernel, ts=ts, sub=SEQ_TILE, lag=EVEN_LAG, seq=seq, d=d, final=final)
    return pl.pallas_call(
        kern,
        grid=(bsz, nt),
        in_specs=[
            pl.BlockSpec((1, HALO, d), lambda b, i: (b, jnp.maximum(i * hb - 1, 0), 0)),
            pl.BlockSpec((1, ts, d), lambda b, i: (b, i, 0)),
            pl.BlockSpec((1, HALO, d), lambda b, i: (b, jnp.minimum((i + 1) * hb, nhb - 1), 0)),
            _const_spec((1, d)),
            _const_spec(w_in.shape),
            _const_spec(conv_w.shape),
            _const_spec(pool_w.shape),
            _const_spec((1, d)),
            _const_spec(w_out.shape),
            _const_spec((1, d)),
        ],
        out_specs=pl.BlockSpec((1, ts, d), lambda b, i: (b, i, 0)),
        out_shape=jax.ShapeDtypeStruct(x.shape, F32),
        compiler_params=_params(("parallel", "arbitrary")),
    )(x, x, x, g.reshape(1, d), w_in, conv_w, pool_w, pool_scale.reshape(1, d), w_out,
      final_g.reshape(1, d))


def _odd_pq_kernel(x_ref, g_ref, wf_ref, ab_ref, pw_ref, qw_ref, *, d):
    hm = _rms(x_ref[0], g_ref[...]).astype(BF16)
    nchunk = x_ref.shape[1] // FFT_N2
    for grp in range(d // GROUP):
        c0 = grp * GROUP
        f = _dot(hm, wf_ref[:, 3 * d + c0:3 * d + c0 + GROUP])
        pq = _dot(f.astype(BF16), ab_ref[grp])
        pw = _words(pq[:, :GROUP])
        qw = _words(pq[:, GROUP:])
        hw = FFT_N2 // 2
        for j in range(nchunk):
            for l in range(GROUP // LANES):
                sl = grp * (GROUP // LANES) + l
                _store_rows(pw_ref.at[0, sl, 0], j, nchunk,
                            pw[j * hw:(j + 1) * hw, l * LANES:(l + 1) * LANES])
                _store_rows(qw_ref.at[0, sl, 0], j, nchunk,
                            qw[j * hw:(j + 1) * hw, l * LANES:(l + 1) * LANES])


def _odd_pq(x, g, w_f, ab):
    bsz, seq, d = x.shape
    ts = PQ_TILE
    nt = seq // ts
    rows = FFT_N2 // 2 * WORD_SUBLANES
    oblk = pl.BlockSpec((1, d // LANES, 1, rows, LANES), lambda b, i: (b, 0, i, 0, 0))
    kern = functools.partial(_odd_pq_kernel, d=d)
    wshape = jax.ShapeDtypeStruct((bsz, d // LANES, nt, rows, LANES), U32)
    return pl.pallas_call(
        kern,
        grid=(bsz, seq // ts),
        in_specs=[pl.BlockSpec((1, ts, d), lambda b, i: (b, i, 0)), _const_spec((1, d)),
                  _const_spec(w_f.shape), _const_spec(ab.shape)],
        out_specs=[oblk, oblk],
        out_shape=[wshape, wshape],
        compiler_params=_params(("parallel", "parallel")),
    )(x, g.reshape(1, d), w_f, ab)


def _odd_tile_rows(n):
    tiles = -(-n // WORD_SUBLANES)
    return WORD_SUBLANES * (tiles if tiles % 2 else tiles + 1)


def _fft_kernel(pw_ref, qw_ref, m1_ref, m2_ref, sw_ref, y_scr, *, n1, mp, nwp):
    nw = FFT_N2 // 2
    nslab = pw_ref.shape[0]
    ws = WORD_SUBLANES

    def slabs(get):
        return jnp.concatenate([get(s) for s in range(nslab)], axis=1)

    def t1_rows(ref, s, w2):
        return jnp.concatenate([ref[s, i, w2 * ws:(w2 + 1) * ws, :] for i in range(n1 // ws)], axis=0)

    for w2 in range(nw):
        rhs = jnp.concatenate([slabs(lambda s: _rows(t1_rows(pw_ref, s, w2))),
                               slabs(lambda s: _rows(t1_rows(qw_ref, s, w2)))], axis=0)
        wds = _words(_dot(m1_ref[w2], rhs))
        for s in range(nslab):
            _store_rows(y_scr.at[s], w2, nwp, wds[:, s * LANES:(s + 1) * LANES])
    m2 = m2_ref[...]
    for m in range(n1 // 2):
        rhs = jnp.concatenate([slabs(lambda s: _rows(y_scr[s, k * nwp:k * nwp + nw, :]))
                               for k in (2 * m, n1 + 2 * m, 2 * m + 1, n1 + 2 * m + 1)], axis=0)
        wds = _words(_dot(m2, rhs))
        for s in range(nslab):
            _store_rows(sw_ref.at[s], m, mp, wds[:, s * LANES:(s + 1) * LANES])
    if mp > n1 // 2:
        zeros = _words(jnp.zeros((2 * (mp - n1 // 2), LANES), F32))
        for s in range(nslab):
            for k2 in range(FFT_N2):
                sw_ref[s, k2 * mp + n1 // 2:(k2 + 1) * mp, :] = zeros


def _fft(pw, qw, m1, m2, n1):
    bsz, nsl, nt, rows, _ = pw.shape
    nw = FFT_N2 // 2
    per = FFT_LANES // LANES
    mp = _odd_tile_rows(n1 // 2)
    nwp = _odd_tile_rows(nw)
    iblk = pl.BlockSpec((None, per, nt, rows, LANES), lambda b, c: (b, c, 0, 0, 0))
    kern = functools.partial(_fft_kernel, n1=n1, mp=mp, nwp=nwp)
    return pl.pallas_call(
        kern,
        grid=(bsz, nsl // per),
        in_specs=[iblk, iblk, _const_spec(m1.shape), _const_spec(m2.shape)],
        out_specs=pl.BlockSpec((None, per, FFT_N2 * mp, LANES), lambda b, c: (b, c, 0, 0)),
        out_shape=jax.ShapeDtypeStruct((bsz, nsl, FFT_N2 * mp, LANES), U32),
        scratch_shapes=[pltpu.VMEM((per, 2 * n1 * nwp, LANES), U32)],
        compiler_params=_params(("parallel", "parallel")),
    )(pw, qw, m1, m2)


def _odd_main_kernel(x_ref, sw_ref, g_ref, win_ref, sg_ref, ws_ref, bs_ref, wout_ref, fg_ref,
                     o_ref, *, ts, sub, lag, d, half_n1, mp, final):
    nhead = d // GROUP
    per = GROUP // LANES
    nk2 = sub // 2 // half_n1

    def stream(k):
        r0 = k * sub
        x = x_ref[0, r0:r0 + sub, :]
        hm = _rms(x, g_ref[...]).astype(BF16)
        ys = [None] * (2 * nhead)

        def v_proj(hd):
            return _dot(hm, win_ref[:, d + hd * GROUP:d + (hd + 1) * GROUP])

        v_next = v_proj(0)
        yield
        for hd in range(nhead):
            c0 = hd * GROUP
            v = v_next
            c_u = _dot(hm, win_ref[:, c0:c0 + GROUP])
            c_z = _dot(hm, win_ref[:, 2 * d + c0:2 * d + c0 + GROUP])
            d_z = _dot(hm, win_ref[:, 4 * d + c0:4 * d + c0 + GROUP])
            if hd + 1 < nhead:
                v_next = v_proj(hd + 1)
            mu = jnp.mean(v, axis=-1, keepdims=True)
            vc = v - mu
            var = jnp.mean(vc * vc, axis=-1, keepdims=True)
            vn = ((vc * lax.rsqrt(var + EPS)) * sg_ref[:, c0:c0 + GROUP]).astype(BF16)
            ws = ws_ref[hd]
            mix = jnp.concatenate(
                [_dot(ws, vn[c * CHUNK:(c + 1) * CHUNK]) + bs_ref[hd] for c in range(sub // CHUNK)],
                axis=0)
            ys[hd] = ((c_u * mix) * _silu(c_z)).astype(BF16)
            spec = jnp.concatenate(
                [_rows(sw_ref[0, hd * per + l, k * nk2 * mp:(k + 1) * nk2 * mp, :]
                       .reshape(nk2, mp, LANES)[:, :half_n1].reshape(sub // 2, LANES))
                 for l in range(per)], axis=1).astype(F32)
            ys[nhead + hd] = (spec * _silu(d_z)).astype(BF16)
            yield
        out = x + _dot(jnp.concatenate(ys, axis=1), wout_ref[...])
        if final:
            out = _rms(out, fg_ref[...])
        o_ref[0, r0:r0 + sub, :] = out
        yield

    _interleave([stream(k) for k in range(ts // sub)], lag)


def _odd_main(x, sw, g, w_main, sgu_g, sgu_ws, sgu_bs, w_out, final_g, final):
    bsz, seq, d = x.shape
    ts = min(MAIN_TILE, seq)
    half_n1 = seq // FFT_N2 // 2
    nk2 = ts // 2 // half_n1
    tile = pl.BlockSpec((1, ts, d), lambda b, i: (b, i, 0))
    mp = sw.shape[2] // FFT_N2
    swblk = pl.BlockSpec((1, sw.shape[1], nk2 * mp, LANES), lambda b, i: (b, 0, i, 0))
    kern = functools.partial(_odd_main_kernel, ts=ts, sub=SEQ_TILE, lag=MAIN_LAG, d=d, half_n1=half_n1,
                             mp=mp, final=final)
    return pl.pallas_call(
        kern,
        grid=(bsz, seq // ts),
        in_specs=[tile, swblk, _const_spec((1, d)),
                  _const_spec(w_main.shape), _const_spec((1, d)), _const_spec(sgu_ws.shape),
                  _const_spec(sgu_bs.shape), _const_spec(w_out.shape), _const_spec((1, d))],
        out_specs=tile,
        out_shape=jax.ShapeDtypeStruct(x.shape, F32),
        compiler_params=_params(("parallel", "parallel")),
    )(x, sw, g.reshape(1, d), w_main, sgu_g.reshape(1, d), sgu_ws, sgu_bs, w_out,
      final_g.reshape(1, d))


def _fold_kernel(c_ref, s_ref, w_ref, ab_ref):
    w = w_ref[0]
    a = jnp.dot(c_ref[...], w, preferred_element_type=F32, precision=lax.Precision.HIGHEST)
    b = jnp.dot(s_ref[...], w, preferred_element_type=F32, precision=lax.Precision.HIGHEST)
    ab_ref[0, :, :GROUP] = a.astype(BF16)
    ab_ref[0, :, GROUP:] = b.astype(BF16)


def _fold_fnet(fnet_w):
    ng, dh, _ = fnet_w.shape
    ang = 2.0 * np.pi * np.outer(np.arange(dh), np.arange(dh)) / dh
    scale = dh ** -0.5
    cc = jnp.asarray(np.cos(ang) * scale, F32)
    sc = jnp.asarray(np.sin(ang) * scale, F32)
    return pl.pallas_call(
        _fold_kernel,
        grid=(ng,),
        in_specs=[pl.BlockSpec((dh, dh), lambda g: (0, 0)), pl.BlockSpec((dh, dh), lambda g: (0, 0)),
                  pl.BlockSpec((1, dh, dh), lambda g: (g, 0, 0))],
        out_specs=pl.BlockSpec((1, dh, 2 * dh), lambda g: (g, 0, 0)),
        out_shape=jax.ShapeDtypeStruct((ng, dh, 2 * dh), BF16),
    )(cc, sc, fnet_w)


def _round_kernel(t_ref, o_ref):
    o_ref[...] = t_ref[...].astype(BF16)


def _round_bf16(table):
    lead = table.shape[0]
    per = max(1, ROUND_BLOCK_BYTES // (4 * int(np.prod(table.shape[1:]))))
    nb = min(lead, per)
    while lead % nb:
        nb -= 1
    blk = pl.BlockSpec((nb,) + table.shape[1:], lambda i: (i,) + (0,) * (table.ndim - 1))
    return pl.pallas_call(_round_kernel, grid=(lead // nb,), in_specs=[blk], out_specs=blk,
                          out_shape=jax.ShapeDtypeStruct(table.shape, BF16))(table)


def _fft_tables(seq, n1):
    n2 = seq // n1
    nw = n2 // 2
    w2 = np.arange(nw)[:, None, None, None]
    e = np.arange(2)[None, None, :, None]
    k1 = np.arange(n1)[None, :, None, None]
    t1 = np.arange(n1)[None, None, None, :]
    ang = 2.0 * np.pi * ((k1 * (t1 * n2 + 2 * w2 + e)) % seq) / seq
    c = np.cos(ang) * n1 ** -0.5
    s = np.sin(ang) * n1 ** -0.5
    m1 = np.zeros((nw, 2, n1, 2, 2, n1, 2))
    for ee in range(2):
        m1[:, 0, :, ee, 0, :, ee] = c[:, :, ee, :]
        m1[:, 0, :, ee, 1, :, ee] = -s[:, :, ee, :]
        m1[:, 1, :, ee, 0, :, ee] = -s[:, :, ee, :]
        m1[:, 1, :, ee, 1, :, ee] = -c[:, :, ee, :]
    m1 = m1.reshape(nw, 4 * n1, 4 * n1)
    ang2 = 2.0 * np.pi * (np.outer(np.arange(n2), np.arange(n2)) % n2) / n2
    m2 = np.zeros((n2, 2, 2, 2, n2))
    for ee in range(2):
        m2[:, ee, ee, 0, :] = np.cos(ang2) * n2 ** -0.5
        m2[:, ee, ee, 1, :] = np.sin(ang2) * n2 ** -0.5
    m2 = m2.reshape(2 * n2, 4 * n2)
    return _round_bf16(jnp.asarray(m1, F32)), _round_bf16(jnp.asarray(m2, F32)[None])[0]


def _odd_layer(x, g, w_in, sgu_g, sgu_ws, sgu_bs, ab, w_out, final_g, final):
    bsz, seq, d = x.shape
    n1 = seq // FFT_N2
    m1, m2 = _fft_tables(seq, n1)
    pw, qw = _odd_pq(x, g, w_in, ab)
    sw = _fft(pw, qw, m1, m2, n1)
    return _odd_main(x, sw, g, w_in, sgu_g, sgu_ws, sgu_bs, w_out, final_g, final)


def _trunk(x, norm_g, ev, od, final_g):
    depth = norm_g.shape[0]
    for i in range(depth):
        j = i // 2
        final = i == depth - 1
        if i % 2 == 0:
            w_in, conv_w, pool_w, pool_scale, w_out = (t[j] for t in ev)
            x = _even_layer(x, norm_g[i], w_in, conv_w, pool_w, pool_scale, w_out, final_g, final)
        else:
            w_in, sgu_g, sgu_ws, sgu_bs, ab, w_out = (t[j] for t in od)
            x = _odd_layer(x, norm_g[i], w_in, sgu_g, sgu_ws, sgu_bs, ab, w_out, final_g, final)
    return x


def kernel(x_prompt, x_sample, norm_g, ev_w_in, ev_conv_w, ev_pool_w, ev_pool_scale, ev_w_out,
           od_w_in, od_sgu_norm_g, od_sgu_ws, od_sgu_bs, od_fnet_w, od_w_out, final_g):
    ev = (ev_w_in.astype(BF16), ev_conv_w, ev_pool_w.astype(BF16), ev_pool_scale,
          ev_w_out.astype(BF16))
    ab = jnp.stack([_fold_fnet(od_fnet_w[j]) for j in range(od_fnet_w.shape[0])])
    od = (od_w_in.astype(BF16), od_sgu_norm_g, od_sgu_ws.astype(BF16),
          od_sgu_bs[..., None], ab, od_w_out.astype(BF16))
    y_prompt = _trunk(x_prompt, norm_g, ev, od, final_g)
    y_sample = _trunk(x_sample, norm_g, ev, od, final_g)
    return (y_prompt, y_sample)
```

```python
import functools

import numpy as np
import jax
import jax.numpy as jnp
from jax import lax
from jax.experimental import pallas as pl
from jax.experimental.pallas import tpu as pltpu

EPS = 1e-6
CONV_K = 3
POOL_WINDOWS = (2, 4, 8, 16)
GROUP = 256
CHUNK = 128
FFT_N2 = 128
FFT_LANES = 256
FFT_WIDE_MAX_N1 = 16
LANES = 128

V7X_VMEM_BYTES = 64 * 1024 * 1024
BF16_SUBLANES = 16
WORD_SUBLANES = 8
MXU_TILE = 256
MXU_SPLIT_MIN_ROWS = 2 * MXU_TILE

HALO = BF16_SUBLANES
SEQ_TILE = 512
EVEN_TILE = 1024
EVEN_LAG = 3
MAIN_TILE = 1024
MAIN_LAG = 1
PQ_STEP_TILES = 2
PQ_TILE = WORD_SUBLANES * FFT_N2
VMEM_LIMIT = V7X_VMEM_BYTES - 8 * 1024 * 1024
ROUND_BLOCK_BYTES = 2 * 1024 * 1024

BF16 = jnp.bfloat16
F32 = jnp.float32
U32 = jnp.uint32


def _dot(a, b):
    m = a.shape[0]
    if m >= MXU_SPLIT_MIN_ROWS and m % (2 * BF16_SUBLANES) == 0:
        h = m // 2
        return jnp.concatenate([jnp.dot(a[:h], b, preferred_element_type=F32),
                                jnp.dot(a[h:], b, preferred_element_type=F32)], axis=0)
    return jnp.dot(a, b, preferred_element_type=F32)


def _rms(x, g):
    ms = jnp.mean(x * x, axis=-1, keepdims=True)
    return (x * lax.rsqrt(ms + EPS)) * g


def _silu(z):
    hz = 0.5 * z
    return hz * jnp.tanh(hz) + hz


def _words(x):
    return pltpu.bitcast(x.astype(BF16), U32)


def _rows(w):
    return pltpu.bitcast(w, BF16)


def _store_rows(ref, j, pitch, val):
    ref[pl.ds(j, val.shape[0], stride=pitch), :] = val


def _interleave(streams, lag):
    live = [True] * len(streams)
    step = 0
    while any(live):
        for k, gen in enumerate(streams):
            if live[k] and step >= k * lag:
                try:
                    next(gen)
                except StopIteration:
                    live[k] = False
        step += 1


def _const_spec(shape):
    nd = len(shape)
    return pl.BlockSpec(shape, lambda *_: (0,) * nd, pipeline_mode=pl.Buffered(1))


def _params(sem):
    return pltpu.CompilerParams(dimension_semantics=sem, vmem_limit_bytes=VMEM_LIMIT)


def _even_kernel(xp_ref, x_ref, xn_ref, g_ref, win_ref, cw_ref, pw_ref, ps_ref, wout_ref,
                 fg_ref, *rest, ts, sub, lag, seq, d, final):
    o_ref = rest[0] if final else rest[1]
    i = pl.program_id(1)
    n = pl.num_programs(1)
    g = g_ref[...]
    w_conv = d
    nblk = d // GROUP
    rows = sub + 2 * HALO
    nsub = ts // sub
    hp = jnp.where(i > 0, _rms(xp_ref[0], g), 0.0).astype(BF16)
    hn = jnp.where(i < n - 1, _rms(xn_ref[0], g), 0.0).astype(BF16)

    def halo(r0):
        return _rms(x_ref[0, r0:r0 + HALO, :], g).astype(BF16)

    def stream(k):
        r0 = k * sub
        x = x_ref[0, r0:r0 + sub, :]
        hm = _rms(x, g).astype(BF16)
        hh = jnp.concatenate([hp if k == 0 else halo(r0 - HALO), hm,
                              hn if k == nsub - 1 else halo(r0 + sub)], axis=0)
        pos = i * ts + r0 + lax.broadcasted_iota(jnp.int32, (sub, 1), 0)
        ys = [None] * (nblk + len(POOL_WINDOWS))

        def conv_dots(cb):
            c0 = cb * GROUP
            return (_dot(hh, win_ref[:, c0:c0 + GROUP]),
                    _dot(hh, win_ref[:, 2 * w_conv + c0:2 * w_conv + c0 + GROUP]),
                    _dot(hm, win_ref[:, w_conv + c0:w_conv + c0 + GROUP]),
                    _dot(hm, win_ref[:, 3 * w_conv + c0:3 * w_conv + c0 + GROUP]))

        def conv_mix(cb, a_h, a_c, a_b, a_z):
            c0 = cb * GROUP
            u = a_c * a_h
            cw = cw_ref[:, c0:c0 + GROUP]
            conv = (pltpu.roll(u, 1, axis=0) * cw[0:1] + u * cw[1:2]
                    + pltpu.roll(u, rows - 1, axis=0) * cw[2:3])
            conv = conv[HALO:HALO + sub]
            ys[cb] = ((a_b * conv) * _silu(a_z)).astype(BF16)

        def pool_dots(grp):
            c0 = 4 * w_conv + grp * GROUP
            return (_dot(hh, win_ref[:, c0:c0 + GROUP]), _dot(hm, win_ref[:, c0 + d:c0 + d + GROUP]))

        def pool_mix(grp, v, b_z):
            win = POOL_WINDOWS[grp]
            q, span = v, 1
            while 2 * span < win:
                q = q + pltpu.roll(q, rows - span, axis=0)
                span *= 2
            s = pltpu.roll(q, span, axis=0) + q
            lo = jnp.maximum(pos - win // 2, 0)
            hi = jnp.minimum(pos + (win - win // 2), seq)
            inv_cnt = 1.0 / (hi - lo).astype(F32)
            dlt = s[HALO:HALO + sub] * inv_cnt - v[HALO:HALO + sub]
            yb = _dot(dlt.astype(BF16), pw_ref[grp])
            yb = yb * ps_ref[:, grp * GROUP:(grp + 1) * GROUP]
            ys[nblk + grp] = (yb * _silu(b_z)).astype(BF16)

        stages = []
        for blk in range(max(nblk, len(POOL_WINDOWS))):
            if blk < nblk:
                stages.append((conv_dots, conv_mix, blk))
            if blk < len(POOL_WINDOWS):
                stages.append((pool_dots, pool_mix, blk))
        ready = stages[0][0](stages[0][2])
        yield
        for j, (_, mix, blk) in enumerate(stages):
            cur = ready
            if j + 1 < len(stages):
                ready = stages[j + 1][0](stages[j + 1][2])
            mix(blk, *cur)
            yield
        out = x + _dot(jnp.concatenate(ys, axis=1), wout_ref[...])
        if final:
            out = _rms(out, fg_ref[...])
        o_ref[0, r0:r0 + sub, :] = out
        if not final:
            rest[2][0, r0:r0 + sub, :] = _rms(out, rest[0][...]).astype(BF16)
        yield

    _interleave([stream(k) for k in range(nsub)], lag)


def _even_layer(x, g, w_in, conv_w, pool_w, pool_scale, w_out, final_g, final, g_next=None):
    bsz, seq, d = x.shape
    ts = min(EVEN_TILE, seq)
    nt = seq // ts
    hb = ts // HALO
    nhb = seq // HALO
    kern = functools.partial(_even_kernel, ts=ts, sub=SEQ_TILE, lag=EVEN_LAG, seq=seq, d=d, final=final)
    tile = pl.BlockSpec((1, ts, d), lambda b, i: (b, i, 0))
    return pl.pallas_call(
        kern,
        grid=(bsz, nt),
        in_specs=[
            pl.BlockSpec((1, HALO, d), lambda b, i: (b, jnp.maximum(i * hb - 1, 0), 0)),
            pl.BlockSpec((1, ts, d), lambda b, i: (b, i, 0)),
            pl.BlockSpec((1, HALO, d), lambda b, i: (b, jnp.minimum((i + 1) * hb, nhb - 1), 0)),
            _const_spec((1, d)),
            _const_spec(w_in.shape),
            _const_spec(conv_w.shape),
            _const_spec(pool_w.shape),
            _const_spec((1, d)),
            _const_spec(w_out.shape),
            _const_spec((1, d)),
        ] + ([] if final else [_const_spec((1, d))]),
        out_specs=tile if final else [tile, tile],
        out_shape=(jax.ShapeDtypeStruct(x.shape, F32) if final else
                   [jax.ShapeDtypeStruct(x.shape, F32), jax.ShapeDtypeStruct(x.shape, BF16)]),
        compiler_params=_params(("parallel", "arbitrary")),
    )(x, x, x, g.reshape(1, d), w_in, conv_w, pool_w, pool_scale.reshape(1, d), w_out,
      final_g.reshape(1, d), *([] if final else [g_next.reshape(1, d)]))


def _odd_pq_kernel(h_ref, wf_ref, ab_ref, pw_ref, qw_ref, *, d):
    hm = h_ref[0]
    nchunk = h_ref.shape[1] // FFT_N2
    for grp in range(d // GROUP):
        c0 = grp * GROUP
        f = _dot(hm, wf_ref[:, 3 * d + c0:3 * d + c0 + GROUP])
        pq = _dot(f.astype(BF16), ab_ref[grp])
        pw = _words(pq[:, :GROUP])
        qw = _words(pq[:, GROUP:])
        hw = FFT_N2 // 2
        for j in range(nchunk):
            wt, jj = divmod(j, WORD_SUBLANES)
            for l in range(GROUP // LANES):
                sl = grp * (GROUP // LANES) + l
                _store_rows(pw_ref.at[0, sl, wt], jj, WORD_SUBLANES,
                            pw[j * hw:(j + 1) * hw, l * LANES:(l + 1) * LANES])
                _store_rows(qw_ref.at[0, sl, wt], jj, WORD_SUBLANES,
                            qw[j * hw:(j + 1) * hw, l * LANES:(l + 1) * LANES])


def _odd_pq(h, w_f, ab):
    bsz, seq, d = h.shape
    ts = PQ_STEP_TILES * PQ_TILE
    rows = FFT_N2 // 2 * WORD_SUBLANES
    oblk = pl.BlockSpec((1, d // LANES, PQ_STEP_TILES, rows, LANES), lambda b, i: (b, 0, i, 0, 0))
    kern = functools.partial(_odd_pq_kernel, d=d)
    wshape = jax.ShapeDtypeStruct((bsz, d // LANES, seq // PQ_TILE, rows, LANES), U32)
    return pl.pallas_call(
        kern,
        grid=(bsz, seq // ts),
        in_specs=[pl.BlockSpec((1, ts, d), lambda b, i: (b, i, 0)),
                  _const_spec(w_f.shape), _const_spec(ab.shape)],
        out_specs=[oblk, oblk],
        out_shape=[wshape, wshape],
        compiler_params=_params(("parallel", "parallel")),
    )(h, w_f, ab)


def _odd_tile_rows(n):
    tiles = -(-n // WORD_SUBLANES)
    return WORD_SUBLANES * (tiles if tiles % 2 else tiles + 1)


def _fft_kernel(pw_ref, qw_ref, m1_ref, m2_ref, sw_ref, y_scr, *, n1, mp, nwp):
    nw = FFT_N2 // 2
    nslab = pw_ref.shape[0]
    ws = WORD_SUBLANES

    def slabs(get):
        return jnp.concatenate([get(s) for s in range(nslab)], axis=1)

    def t1_rows(ref, s, w2):
        return jnp.concatenate([ref[s, i, w2 * ws:(w2 + 1) * ws, :] for i in range(n1 // ws)], axis=0)

    for w2 in range(nw):
        rhs = jnp.concatenate([slabs(lambda s: _rows(t1_rows(pw_ref, s, w2))),
                               slabs(lambda s: _rows(t1_rows(qw_ref, s, w2)))], axis=0)
        wds = _words(_dot(m1_ref[w2], rhs))
        for s in range(nslab):
            _store_rows(y_scr.at[s], w2, nwp, wds[:, s * LANES:(s + 1) * LANES])
    m2 = m2_ref[...]
    for m in range(n1 // 2):
        rhs = jnp.concatenate([slabs(lambda s: _rows(y_scr[s, k * nwp:k * nwp + nw, :]))
                               for k in (2 * m, n1 + 2 * m, 2 * m + 1, n1 + 2 * m + 1)], axis=0)
        wds = _words(_dot(m2, rhs))
        for s in range(nslab):
            _store_rows(sw_ref.at[s], m, mp, wds[:, s * LANES:(s + 1) * LANES])
    if mp > n1 // 2:
        zeros = _words(jnp.zeros((2 * (mp - n1 // 2), LANES), F32))
        for s in range(nslab):
            for k2 in range(FFT_N2):
                sw_ref[s, k2 * mp + n1 // 2:(k2 + 1) * mp, :] = zeros


def _fft(pw, qw, m1, m2, n1):
    bsz, nsl, nt, rows, _ = pw.shape
    nw = FFT_N2 // 2
    per = (FFT_LANES if n1 > FFT_WIDE_MAX_N1 else 2 * FFT_LANES) // LANES
    mp = _odd_tile_rows(n1 // 2)
    nwp = _odd_tile_rows(nw)
    iblk = pl.BlockSpec((None, per, nt, rows, LANES), lambda b, c: (b, c, 0, 0, 0))
    kern = functools.partial(_fft_kernel, n1=n1, mp=mp, nwp=nwp)
    return pl.pallas_call(
        kern,
        grid=(bsz, nsl // per),
        in_specs=[iblk, iblk, _const_spec(m1.shape), _const_spec(m2.shape)],
        out_specs=pl.BlockSpec((None, per, FFT_N2 * mp, LANES), lambda b, c: (b, c, 0, 0)),
        out_shape=jax.ShapeDtypeStruct((bsz, nsl, FFT_N2 * mp, LANES), U32),
        scratch_shapes=[pltpu.VMEM((per, 2 * n1 * nwp, LANES), U32)],
        compiler_params=_params(("parallel", "parallel")),
    )(pw, qw, m1, m2)


def _odd_main_kernel(x_ref, h_ref, sw_ref, win_ref, sg_ref, ws_ref, bs_ref, wout_ref, fg_ref,
                     o_ref, *, ts, sub, lag, d, half_n1, mp, final):
    nhead = d // GROUP
    per = GROUP // LANES
    nk2 = sub // 2 // half_n1

    def stream(k):
        r0 = k * sub
        x = x_ref[0, r0:r0 + sub, :]
        hm = h_ref[0, r0:r0 + sub, :]
        ys = [None] * (2 * nhead)

        def v_proj(hd):
            return _dot(hm, win_ref[:, d + hd * GROUP:d + (hd + 1) * GROUP])

        v_next = v_proj(0)
        yield
        for hd in range(nhead):
            c0 = hd * GROUP
            v = v_next
            c_u = _dot(hm, win_ref[:, c0:c0 + GROUP])
            c_z = _dot(hm, win_ref[:, 2 * d + c0:2 * d + c0 + GROUP])
            d_z = _dot(hm, win_ref[:, 4 * d + c0:4 * d + c0 + GROUP])
            if hd + 1 < nhead:
                v_next = v_proj(hd + 1)
            mu = jnp.mean(v, axis=-1, keepdims=True)
            vc = v - mu
            var = jnp.mean(vc * vc, axis=-1, keepdims=True)
            vn = ((vc * lax.rsqrt(var + EPS)) * sg_ref[:, c0:c0 + GROUP]).astype(BF16)
            ws = ws_ref[hd]
            mix = jnp.concatenate(
                [_dot(ws, vn[c * CHUNK:(c + 1) * CHUNK]) + bs_ref[hd] for c in range(sub // CHUNK)],
                axis=0)
            ys[hd] = ((c_u * mix) * _silu(c_z)).astype(BF16)
            spec = jnp.concatenate(
                [_rows(sw_ref[0, hd * per + l, k * nk2 * mp:(k + 1) * nk2 * mp, :]
                       .reshape(nk2, mp, LANES)[:, :half_n1].reshape(sub // 2, LANES))
                 for l in range(per)], axis=1).astype(F32)
            ys[nhead + hd] = (spec * _silu(d_z)).astype(BF16)
            yield
        out = x + _dot(jnp.concatenate(ys, axis=1), wout_ref[...])
        if final:
            out = _rms(out, fg_ref[...])
        o_ref[0, r0:r0 + sub, :] = out
        yield

    _interleave([stream(k) for k in range(ts // sub)], lag)


def _odd_main(x, h, sw, w_main, sgu_g, sgu_ws, sgu_bs, w_out, final_g, final):
    bsz, seq, d = x.shape
    ts = min(MAIN_TILE, seq)
    half_n1 = seq // FFT_N2 // 2
    nk2 = ts // 2 // half_n1
    tile = pl.BlockSpec((1, ts, d), lambda b, i: (b, i, 0))
    mp = sw.shape[2] // FFT_N2
    swblk = pl.BlockSpec((1, sw.shape[1], nk2 * mp, LANES), lambda b, i: (b, 0, i, 0))
    kern = functools.partial(_odd_main_kernel, ts=ts, sub=SEQ_TILE, lag=MAIN_LAG, d=d, half_n1=half_n1,
                             mp=mp, final=final)
    return pl.pallas_call(
        kern,
        grid=(bsz, seq // ts),
        in_specs=[tile, tile, swblk,
                  _const_spec(w_main.shape), _const_spec((1, d)), _const_spec(sgu_ws.shape),
                  _const_spec(sgu_bs.shape), _const_spec(w_out.shape), _const_spec((1, d))],
        out_specs=tile,
        out_shape=jax.ShapeDtypeStruct(x.shape, F32),
        compiler_params=_params(("parallel", "parallel")),
    )(x, h, sw, w_main, sgu_g.reshape(1, d), sgu_ws, sgu_bs, w_out,
      final_g.reshape(1, d))


def _fold_kernel(c_ref, s_ref, w_ref, ab_ref):
    w = w_ref[0]
    a = jnp.dot(c_ref[...], w, preferred_element_type=F32, precision=lax.Precision.HIGHEST)
    b = jnp.dot(s_ref[...], w, preferred_element_type=F32, precision=lax.Precision.HIGHEST)
    ab_ref[0, :, :GROUP] = a.astype(BF16)
    ab_ref[0, :, GROUP:] = b.astype(BF16)


def _fold_fnet(fnet_w):
    ng, dh, _ = fnet_w.shape
    ang = 2.0 * np.pi * np.outer(np.arange(dh), np.arange(dh)) / dh
    scale = dh ** -0.5
    cc = jnp.asarray(np.cos(ang) * scale, F32)
    sc = jnp.asarray(np.sin(ang) * scale, F32)
    return pl.pallas_call(
        _fold_kernel,
        grid=(ng,),
        in_specs=[pl.BlockSpec((dh, dh), lambda g: (0, 0)), pl.BlockSpec((dh, dh), lambda g: (0, 0)),
                  pl.BlockSpec((1, dh, dh), lambda g: (g, 0, 0))],
        out_specs=pl.BlockSpec((1, dh, 2 * dh), lambda g: (g, 0, 0)),
        out_shape=jax.ShapeDtypeStruct((ng, dh, 2 * dh), BF16),
    )(cc, sc, fnet_w)


def _round_kernel(t_ref, o_ref):
    o_ref[...] = t_ref[...].astype(BF16)


def _round_bf16(table):
    lead = table.shape[0]
    per = max(1, ROUND_BLOCK_BYTES // (4 * int(np.prod(table.shape[1:]))))
    nb = min(lead, per)
    while lead % nb:
        nb -= 1
    blk = pl.BlockSpec((nb,) + table.shape[1:], lambda i: (i,) + (0,) * (table.ndim - 1))
    return pl.pallas_call(_round_kernel, grid=(lead // nb,), in_specs=[blk], out_specs=blk,
                          out_shape=jax.ShapeDtypeStruct(table.shape, BF16))(table)


def _fft_tables(seq, n1):
    n2 = seq // n1
    nw = n2 // 2
    w2 = np.arange(nw)[:, None, None, None]
    e = np.arange(2)[None, None, :, None]
    k1 = np.arange(n1)[None, :, None, None]
    t1 = np.arange(n1)[None, None, None, :]
    ang = 2.0 * np.pi * ((k1 * (t1 * n2 + 2 * w2 + e)) % seq) / seq
    c = np.cos(ang) * n1 ** -0.5
    s = np.sin(ang) * n1 ** -0.5
    m1 = np.zeros((nw, 2, n1, 2, 2, n1, 2))
    for ee in range(2):
        m1[:, 0, :, ee, 0, :, ee] = c[:, :, ee, :]
        m1[:, 0, :, ee, 1, :, ee] = -s[:, :, ee, :]
        m1[:, 1, :, ee, 0, :, ee] = -s[:, :, ee, :]
        m1[:, 1, :, ee, 1, :, ee] = -c[:, :, ee, :]
    m1 = m1.reshape(nw, 4 * n1, 4 * n1)
    ang2 = 2.0 * np.pi * (np.outer(np.arange(n2), np.arange(n2)) % n2) / n2
    m2 = np.zeros((n2, 2, 2, 2, n2))
    for ee in range(2):
        m2[:, ee, ee, 0, :] = np.cos(ang2) * n2 ** -0.5
        m2[:, ee, ee, 1, :] = np.sin(ang2) * n2 ** -0.5
    m2 = m2.reshape(2 * n2, 4 * n2)
    return _round_bf16(jnp.asarray(m1, F32)), _round_bf16(jnp.asarray(m2, F32)[None])[0]


def _odd_layer(x, h, w_in, sgu_g, sgu_ws, sgu_bs, ab, w_out, final_g, final):
    bsz, seq, d = x.shape
    n1 = seq // FFT_N2
    m1, m2 = _fft_tables(seq, n1)
    pw, qw = _odd_pq(h, w_in, ab)
    sw = _fft(pw, qw, m1, m2, n1)
    return _odd_main(x, h, sw, w_in, sgu_g, sgu_ws, sgu_bs, w_out, final_g, final)


def _trunk(x, norm_g, ev, od, final_g):
    depth = norm_g.shape[0]
    for i in range(depth):
        j = i // 2
        final = i == depth - 1
        if i % 2 == 0:
            w_in, conv_w, pool_w, pool_scale, w_out = (t[j] for t in ev)
            if final:
                x = _even_layer(x, norm_g[i], w_in, conv_w, pool_w, pool_scale, w_out, final_g, final)
            else:
                x, h = _even_layer(x, norm_g[i], w_in, conv_w, pool_w, pool_scale, w_out, final_g, final,
                                   norm_g[i + 1])
        else:
            w_in, sgu_g, sgu_ws, sgu_bs, ab, w_out = (t[j] for t in od)
            x = _odd_layer(x, h, w_in, sgu_g, sgu_ws, sgu_bs, ab, w_out, final_g, final)
    return x


def kernel(x_prompt, x_sample, norm_g, ev_w_in, ev_conv_w, ev_pool_w, ev_pool_scale, ev_w_out,
           od_w_in, od_sgu_norm_g, od_sgu_ws, od_sgu_bs, od_fnet_w, od_w_out, final_g):
    ev = (ev_w_in.astype(BF16), ev_conv_w, ev_pool_w.astype(BF16), ev_pool_scale,
          ev_w_out.astype(BF16))
    ab = jnp.stack([_fold_fnet(od_fnet_w[j]) for j in range(od_fnet_w.shape[0])])
    od = (od_w_in.astype(BF16), od_sgu_norm_g, od_sgu_ws.astype(BF16),
          od_sgu_bs[..., None], ab, od_w_out.astype(BF16))
    y_prompt = _trunk(x_prompt, norm_g, ev, od, final_g)
    y_sample = _trunk(x_sample, norm_g, ev, od, final_g)
    return (y_prompt, y_sample)
```

```python
import functools

import numpy as np
import jax
import jax.numpy as jnp
from jax import lax
from jax.experimental import pallas as pl
from jax.experimental.pallas import tpu as pltpu

EPS = 1e-6
CONV_K = 3
POOL_WINDOWS = (2, 4, 8, 16)
GROUP = 256
CHUNK = 128
FFT_N2 = 128
FFT_LANES = 256
FFT_WIDE_MAX_N1 = 16
LANES = 128

V7X_VMEM_BYTES = 64 * 1024 * 1024
BF16_SUBLANES = 16
WORD_SUBLANES = 8
MXU_TILE = 256
MXU_SPLIT_MIN_ROWS = 2 * MXU_TILE

HALO = BF16_SUBLANES
SEQ_TILE = 512
EVEN_TILE = 1024
EVEN_LAG = 3
MAIN_TILE = 1024
MAIN_LAG = 1
PQ_STEP_TILES = 2
PQ_TILE = WORD_SUBLANES * FFT_N2
VMEM_LIMIT = V7X_VMEM_BYTES - 8 * 1024 * 1024

BF16 = jnp.bfloat16
F32 = jnp.float32
U32 = jnp.uint32


def _dot(a, b):
    m = a.shape[0]
    if m >= MXU_SPLIT_MIN_ROWS and m % (2 * BF16_SUBLANES) == 0:
        h = m // 2
        return jnp.concatenate([jnp.dot(a[:h], b, preferred_element_type=F32),
                                jnp.dot(a[h:], b, preferred_element_type=F32)], axis=0)
    return jnp.dot(a, b, preferred_element_type=F32)


def _rms(x, g):
    ms = jnp.mean(x * x, axis=-1, keepdims=True)
    return (x * lax.rsqrt(ms + EPS)) * g


def _silu(z):
    hz = 0.5 * z
    return hz * jnp.tanh(hz) + hz


def _words(x):
    return pltpu.bitcast(x.astype(BF16), U32)


def _rows(w):
    return pltpu.bitcast(w, BF16)


def _store_rows(ref, j, pitch, val):
    ref[pl.ds(j, val.shape[0], stride=pitch), :] = val


def _interleave(streams, lag):
    live = [True] * len(streams)
    step = 0
    while any(live):
        for k, gen in enumerate(streams):
            if live[k] and step >= k * lag:
                try:
                    next(gen)
                except StopIteration:
                    live[k] = False
        step += 1


def _const_spec(shape):
    nd = len(shape)
    return pl.BlockSpec(shape, lambda *_: (0,) * nd, pipeline_mode=pl.Buffered(1))


def _params(sem):
    return pltpu.CompilerParams(dimension_semantics=sem, vmem_limit_bytes=VMEM_LIMIT)


def _even_kernel(xp_ref, x_ref, xn_ref, g_ref, win_ref, cw_ref, pw_ref, ps_ref, wout_ref,
                 fg_ref, *rest, ts, sub, lag, seq, d, final):
    o_ref = rest[0] if final else rest[1]
    i = pl.program_id(1)
    n = pl.num_programs(1)
    g = g_ref[...]
    w_conv = d
    nblk = d // GROUP
    rows = sub + 2 * HALO
    nsub = ts // sub
    hp = jnp.where(i > 0, _rms(xp_ref[0], g), 0.0).astype(BF16)
    hn = jnp.where(i < n - 1, _rms(xn_ref[0], g), 0.0).astype(BF16)

    def halo(r0):
        return _rms(x_ref[0, r0:r0 + HALO, :], g).astype(BF16)

    def stream(k):
        r0 = k * sub
        x = x_ref[0, r0:r0 + sub, :]
        hm = _rms(x, g).astype(BF16)
        hh = jnp.concatenate([hp if k == 0 else halo(r0 - HALO), hm,
                              hn if k == nsub - 1 else halo(r0 + sub)], axis=0)
        pos = i * ts + r0 + lax.broadcasted_iota(jnp.int32, (sub, 1), 0)
        ys = [None] * (nblk + len(POOL_WINDOWS))

        def conv_dots(cb):
            c0 = cb * GROUP
            return (_dot(hh, win_ref[:, c0:c0 + GROUP]),
                    _dot(hh, win_ref[:, 2 * w_conv + c0:2 * w_conv + c0 + GROUP]),
                    _dot(hm, win_ref[:, w_conv + c0:w_conv + c0 + GROUP]),
                    _dot(hm, win_ref[:, 3 * w_conv + c0:3 * w_conv + c0 + GROUP]))

        def conv_mix(cb, a_h, a_c, a_b, a_z):
            c0 = cb * GROUP
            u = a_c * a_h
            cw = cw_ref[:, c0:c0 + GROUP]
            conv = (pltpu.roll(u, 1, axis=0) * cw[0:1] + u * cw[1:2]
                    + pltpu.roll(u, rows - 1, axis=0) * cw[2:3])
            conv = conv[HALO:HALO + sub]
            ys[cb] = ((a_b * conv) * _silu(a_z)).astype(BF16)

        def pool_dots(grp):
            c0 = 4 * w_conv + grp * GROUP
            return (_dot(hh, win_ref[:, c0:c0 + GROUP]), _dot(hm, win_ref[:, c0 + d:c0 + d + GROUP]))

        def pool_mix(grp, v, b_z):
            win = POOL_WINDOWS[grp]
            q, span = v, 1
            while 2 * span < win:
                q = q + pltpu.roll(q, rows - span, axis=0)
                span *= 2
            s = pltpu.roll(q, span, axis=0) + q
            lo = jnp.maximum(pos - win // 2, 0)
            hi = jnp.minimum(pos + (win - win // 2), seq)
            inv_cnt = 1.0 / (hi - lo).astype(F32)
            dlt = s[HALO:HALO + sub] * inv_cnt - v[HALO:HALO + sub]
            yb = _dot(dlt.astype(BF16), pw_ref[grp].astype(BF16))
            yb = yb * ps_ref[:, grp * GROUP:(grp + 1) * GROUP]
            ys[nblk + grp] = (yb * _silu(b_z)).astype(BF16)

        stages = []
        for blk in range(max(nblk, len(POOL_WINDOWS))):
            if blk < nblk:
                stages.append((conv_dots, conv_mix, blk))
            if blk < len(POOL_WINDOWS):
                stages.append((pool_dots, pool_mix, blk))
        ready = stages[0][0](stages[0][2])
        yield
        for j, (_, mix, blk) in enumerate(stages):
            cur = ready
            if j + 1 < len(stages):
                ready = stages[j + 1][0](stages[j + 1][2])
            mix(blk, *cur)
            yield
        out = x + _dot(jnp.concatenate(ys, axis=1), wout_ref[...])
        if final:
            out = _rms(out, fg_ref[...])
        o_ref[0, r0:r0 + sub, :] = out
        if not final:
            rest[2][0, r0:r0 + sub, :] = _rms(out, rest[0][...]).astype(BF16)
        yield

    _interleave([stream(k) for k in range(nsub)], lag)


def _even_layer(x, g, w_in, conv_w, pool_w, pool_scale, w_out, final_g, final, g_next=None):
    bsz, seq, d = x.shape
    ts = min(EVEN_TILE, seq)
    nt = seq // ts
    hb = ts // HALO
    nhb = seq // HALO
    kern = functools.partial(_even_kernel, ts=ts, sub=SEQ_TILE, lag=EVEN_LAG, seq=seq, d=d, final=final)
    tile = pl.BlockSpec((1, ts, d), lambda b, i: (b, i, 0))
    return pl.pallas_call(
        kern,
        grid=(bsz, nt),
        in_specs=[
            pl.BlockSpec((1, HALO, d), lambda b, i: (b, jnp.maximum(i * hb - 1, 0), 0)),
            pl.BlockSpec((1, ts, d), lambda b, i: (b, i, 0)),
            pl.BlockSpec((1, HALO, d), lambda b, i: (b, jnp.minimum((i + 1) * hb, nhb - 1), 0)),
            _const_spec((1, d)),
            _const_spec(w_in.shape),
            _const_spec(conv_w.shape),
            _const_spec(pool_w.shape),
            _const_spec((1, d)),
            _const_spec(w_out.shape),
            _const_spec((1, d)),
        ] + ([] if final else [_const_spec((1, d))]),
        out_specs=tile if final else [tile, tile],
        out_shape=(jax.ShapeDtypeStruct(x.shape, F32) if final else
                   [jax.ShapeDtypeStruct(x.shape, F32), jax.ShapeDtypeStruct(x.shape, BF16)]),
        compiler_params=_params(("parallel", "arbitrary")),
    )(x, x, x, g.reshape(1, d), w_in, conv_w, pool_w, pool_scale.reshape(1, d), w_out,
      final_g.reshape(1, d), *([] if final else [g_next.reshape(1, d)]))


def _odd_pq_kernel(h_ref, wf_ref, ab_ref, pw_ref, qw_ref, *, d):
    hm = h_ref[0]
    nchunk = h_ref.shape[1] // FFT_N2
    for grp in range(d // GROUP):
        c0 = grp * GROUP
        f = _dot(hm, wf_ref[:, 3 * d + c0:3 * d + c0 + GROUP])
        pq = _dot(f.astype(BF16), ab_ref[grp])
        pw = _words(pq[:, :GROUP])
        qw = _words(pq[:, GROUP:])
        hw = FFT_N2 // 2
        for j in range(nchunk):
            wt, jj = divmod(j, WORD_SUBLANES)
            for l in range(GROUP // LANES):
                sl = grp * (GROUP // LANES) + l
                _store_rows(pw_ref.at[0, sl, wt], jj, WORD_SUBLANES,
                            pw[j * hw:(j + 1) * hw, l * LANES:(l + 1) * LANES])
                _store_rows(qw_ref.at[0, sl, wt], jj, WORD_SUBLANES,
                            qw[j * hw:(j + 1) * hw, l * LANES:(l + 1) * LANES])


def _odd_pq(h, w_f, ab):
    bsz, seq, d = h.shape
    ts = PQ_STEP_TILES * PQ_TILE
    rows = FFT_N2 // 2 * WORD_SUBLANES
    oblk = pl.BlockSpec((1, d // LANES, PQ_STEP_TILES, rows, LANES), lambda b, i: (b, 0, i, 0, 0))
    kern = functools.partial(_odd_pq_kernel, d=d)
    wshape = jax.ShapeDtypeStruct((bsz, d // LANES, seq // PQ_TILE, rows, LANES), U32)
    return pl.pallas_call(
        kern,
        grid=(bsz, seq // ts),
        in_specs=[pl.BlockSpec((1, ts, d), lambda b, i: (b, i, 0)),
                  _const_spec(w_f.shape), _const_spec(ab.shape)],
        out_specs=[oblk, oblk],
        out_shape=[wshape, wshape],
        compiler_params=_params(("parallel", "parallel")),
    )(h, w_f, ab)


def _odd_tile_rows(n):
    tiles = -(-n // WORD_SUBLANES)
    return WORD_SUBLANES * (tiles if tiles % 2 else tiles + 1)


def _fft_kernel(pw_ref, qw_ref, m1_ref, m2_ref, sw_ref, y_scr, *, n1, mp, nwp):
    nw = FFT_N2 // 2
    nslab = pw_ref.shape[0]
    ws = WORD_SUBLANES

    def slabs(get):
        return jnp.concatenate([get(s) for s in range(nslab)], axis=1)

    def t1_rows(ref, s, w2):
        return jnp.concatenate([ref[s, i, w2 * ws:(w2 + 1) * ws, :] for i in range(n1 // ws)], axis=0)

    for w2 in range(nw):
        rhs = jnp.concatenate([slabs(lambda s: _rows(t1_rows(pw_ref, s, w2))),
                               slabs(lambda s: _rows(t1_rows(qw_ref, s, w2)))], axis=0)
        wds = _words(_dot(m1_ref[w2].astype(BF16), rhs))
        for s in range(nslab):
            _store_rows(y_scr.at[s], w2, nwp, wds[:, s * LANES:(s + 1) * LANES])
    m2 = m2_ref[...].astype(BF16)
    for m in range(n1 // 2):
        rhs = jnp.concatenate([slabs(lambda s: _rows(y_scr[s, k * nwp:k * nwp + nw, :]))
                               for k in (2 * m, n1 + 2 * m, 2 * m + 1, n1 + 2 * m + 1)], axis=0)
        wds = _words(_dot(m2, rhs))
        for s in range(nslab):
            _store_rows(sw_ref.at[s], m, mp, wds[:, s * LANES:(s + 1) * LANES])
    if mp > n1 // 2:
        zeros = _words(jnp.zeros((2 * (mp - n1 // 2), LANES), F32))
        for s in range(nslab):
            for k2 in range(FFT_N2):
                sw_ref[s, k2 * mp + n1 // 2:(k2 + 1) * mp, :] = zeros


def _fft(pw, qw, m1, m2, n1):
    bsz, nsl, nt, rows, _ = pw.shape
    nw = FFT_N2 // 2
    per = (FFT_LANES if n1 > FFT_WIDE_MAX_N1 else 2 * FFT_LANES) // LANES
    mp = _odd_tile_rows(n1 // 2)
    nwp = _odd_tile_rows(nw)
    iblk = pl.BlockSpec((None, per, nt, rows, LANES), lambda b, c: (b, c, 0, 0, 0))
    kern = functools.partial(_fft_kernel, n1=n1, mp=mp, nwp=nwp)
    return pl.pallas_call(
        kern,
        grid=(bsz, nsl // per),
        in_specs=[iblk, iblk, _const_spec(m1.shape), _const_spec(m2.shape)],
        out_specs=pl.BlockSpec((None, per, FFT_N2 * mp, LANES), lambda b, c: (b, c, 0, 0)),
        out_shape=jax.ShapeDtypeStruct((bsz, nsl, FFT_N2 * mp, LANES), U32),
        scratch_shapes=[pltpu.VMEM((per, 2 * n1 * nwp, LANES), U32)],
        compiler_params=_params(("parallel", "parallel")),
    )(pw, qw, m1, m2)


def _odd_main_kernel(x_ref, h_ref, sw_ref, win_ref, sg_ref, ws_ref, bs_ref, wout_ref, fg_ref,
                     o_ref, *, ts, sub, lag, d, half_n1, mp, final):
    nhead = d // GROUP
    per = GROUP // LANES
    nk2 = sub // 2 // half_n1

    def stream(k):
        r0 = k * sub
        x = x_ref[0, r0:r0 + sub, :]
        hm = h_ref[0, r0:r0 + sub, :]
        ys = [None] * (2 * nhead)

        def v_proj(hd):
            return _dot(hm, win_ref[:, d + hd * GROUP:d + (hd + 1) * GROUP])

        v_next = v_proj(0)
        yield
        for hd in range(nhead):
            c0 = hd * GROUP
            v = v_next
            c_u = _dot(hm, win_ref[:, c0:c0 + GROUP])
            c_z = _dot(hm, win_ref[:, 2 * d + c0:2 * d + c0 + GROUP])
            d_z = _dot(hm, win_ref[:, 4 * d + c0:4 * d + c0 + GROUP])
            if hd + 1 < nhead:
                v_next = v_proj(hd + 1)
            mu = jnp.mean(v, axis=-1, keepdims=True)
            vc = v - mu
            var = jnp.mean(vc * vc, axis=-1, keepdims=True)
            vn = ((vc * lax.rsqrt(var + EPS)) * sg_ref[:, c0:c0 + GROUP]).astype(BF16)
            ws = ws_ref[hd].astype(BF16)
            mix = jnp.concatenate(
                [_dot(ws, vn[c * CHUNK:(c + 1) * CHUNK]) + bs_ref[hd] for c in range(sub // CHUNK)],
                axis=0)
            ys[hd] = ((c_u * mix) * _silu(c_z)).astype(BF16)
            spec = jnp.concatenate(
                [_rows(sw_ref[0, hd * per + l, k * nk2 * mp:(k + 1) * nk2 * mp, :]
                       .reshape(nk2, mp, LANES)[:, :half_n1].reshape(sub // 2, LANES))
                 for l in range(per)], axis=1).astype(F32)
            ys[nhead + hd] = (spec * _silu(d_z)).astype(BF16)
            yield
        out = x + _dot(jnp.concatenate(ys, axis=1), wout_ref[...])
        if final:
            out = _rms(out, fg_ref[...])
        o_ref[0, r0:r0 + sub, :] = out
        yield

    _interleave([stream(k) for k in range(ts // sub)], lag)


def _odd_main(x, h, sw, w_main, sgu_g, sgu_ws, sgu_bs, w_out, final_g, final):
    bsz, seq, d = x.shape
    ts = min(MAIN_TILE, seq)
    half_n1 = seq // FFT_N2 // 2
    nk2 = ts // 2 // half_n1
    tile = pl.BlockSpec((1, ts, d), lambda b, i: (b, i, 0))
    mp = sw.shape[2] // FFT_N2
    swblk = pl.BlockSpec((1, sw.shape[1], nk2 * mp, LANES), lambda b, i: (b, 0, i, 0))
    kern = functools.partial(_odd_main_kernel, ts=ts, sub=SEQ_TILE, lag=MAIN_LAG, d=d, half_n1=half_n1,
                             mp=mp, final=final)
    return pl.pallas_call(
        kern,
        grid=(bsz, seq // ts),
        in_specs=[tile, tile, swblk,
                  _const_spec(w_main.shape), _const_spec((1, d)), _const_spec(sgu_ws.shape),
                  _const_spec(sgu_bs.shape), _const_spec(w_out.shape), _const_spec((1, d))],
        out_specs=tile,
        out_shape=jax.ShapeDtypeStruct(x.shape, F32),
        compiler_params=_params(("parallel", "parallel")),
    )(x, h, sw, w_main, sgu_g.reshape(1, d), sgu_ws, sgu_bs, w_out,
      final_g.reshape(1, d))


def _fold_kernel(c_ref, s_ref, w_ref, ab_ref):
    for grp in range(w_ref.shape[0]):
        w = w_ref[grp]
        a = jnp.dot(c_ref[...], w, preferred_element_type=F32, precision=lax.Precision.HIGHEST)
        b = jnp.dot(s_ref[...], w, preferred_element_type=F32, precision=lax.Precision.HIGHEST)
        ab_ref[grp, :, :GROUP] = a.astype(BF16)
        ab_ref[grp, :, GROUP:] = b.astype(BF16)


def _fold_fnet(fnet_w):
    ng, dh, _ = fnet_w.shape
    ang = 2.0 * np.pi * np.outer(np.arange(dh), np.arange(dh)) / dh
    scale = dh ** -0.5
    cc = jnp.asarray(np.cos(ang) * scale, F32)
    sc = jnp.asarray(np.sin(ang) * scale, F32)
    return pl.pallas_call(
        _fold_kernel,
        out_shape=jax.ShapeDtypeStruct((ng, dh, 2 * dh), BF16),
    )(cc, sc, fnet_w)


def _fft_tables(seq, n1):
    n2 = seq // n1
    nw = n2 // 2
    w2 = np.arange(nw)[:, None, None, None]
    e = np.arange(2)[None, None, :, None]
    k1 = np.arange(n1)[None, :, None, None]
    t1 = np.arange(n1)[None, None, None, :]
    ang = 2.0 * np.pi * ((k1 * (t1 * n2 + 2 * w2 + e)) % seq) / seq
    c = np.cos(ang) * n1 ** -0.5
    s = np.sin(ang) * n1 ** -0.5
    m1 = np.zeros((nw, 2, n1, 2, 2, n1, 2))
    for ee in range(2):
        m1[:, 0, :, ee, 0, :, ee] = c[:, :, ee, :]
        m1[:, 0, :, ee, 1, :, ee] = -s[:, :, ee, :]
        m1[:, 1, :, ee, 0, :, ee] = -s[:, :, ee, :]
        m1[:, 1, :, ee, 1, :, ee] = -c[:, :, ee, :]
    m1 = m1.reshape(nw, 4 * n1, 4 * n1)
    ang2 = 2.0 * np.pi * (np.outer(np.arange(n2), np.arange(n2)) % n2) / n2
    m2 = np.zeros((n2, 2, 2, 2, n2))
    for ee in range(2):
        m2[:, ee, ee, 0, :] = np.cos(ang2) * n2 ** -0.5
        m2[:, ee, ee, 1, :] = np.sin(ang2) * n2 ** -0.5
    m2 = m2.reshape(2 * n2, 4 * n2)
    return jnp.asarray(m1, F32), jnp.asarray(m2, F32)


def _odd_layer(x, h, w_in, sgu_g, sgu_ws, sgu_bs, ab, w_out, final_g, final):
    bsz, seq, d = x.shape
    n1 = seq // FFT_N2
    m1, m2 = _fft_tables(seq, n1)
    pw, qw = _odd_pq(h, w_in, ab)
    sw = _fft(pw, qw, m1, m2, n1)
    return _odd_main(x, h, sw, w_in, sgu_g, sgu_ws, sgu_bs, w_out, final_g, final)


def _trunk(x, norm_g, ev, od, final_g):
    depth = norm_g.shape[0]
    for i in range(depth):
        j = i // 2
        final = i == depth - 1
        if i % 2 == 0:
            w_in, conv_w, pool_w, pool_scale, w_out = (t[j] for t in ev)
            if final:
                x = _even_layer(x, norm_g[i], w_in, conv_w, pool_w, pool_scale, w_out, final_g, final)
            else:
                x, h = _even_layer(x, norm_g[i], w_in, conv_w, pool_w, pool_scale, w_out, final_g, final,
                                   norm_g[i + 1])
        else:
            w_in, sgu_g, sgu_ws, sgu_bs, ab, w_out = (t[j] for t in od)
            x = _odd_layer(x, h, w_in, sgu_g, sgu_ws, sgu_bs, ab, w_out, final_g, final)
    return x


def kernel(x_prompt, x_sample, norm_g, ev_w_in, ev_conv_w, ev_pool_w, ev_pool_scale, ev_w_out,
           od_w_in, od_sgu_norm_g, od_sgu_ws, od_sgu_bs, od_fnet_w, od_w_out, final_g):
    ev = (ev_w_in.astype(BF16), ev_conv_w, ev_pool_w, ev_pool_scale, ev_w_out.astype(BF16))
    ab = [_fold_fnet(od_fnet_w[j]) for j in range(od_fnet_w.shape[0])]
    od = (od_w_in.astype(BF16), od_sgu_norm_g, od_sgu_ws, od_sgu_bs[..., None], ab,
          od_w_out.astype(BF16))
    y_prompt = _trunk(x_prompt, norm_g, ev, od, final_g)
    y_sample = _trunk(x_sample, norm_g, ev, od, final_g)
    return (y_prompt, y_sample)
```

```python
import functools

import numpy as np
import jax
import jax.numpy as jnp
from jax import lax
from jax.experimental import pallas as pl
from jax.experimental.pallas import tpu as pltpu

EPS = 1e-6
CONV_K = 3
POOL_WINDOWS = (2, 4, 8, 16)
GROUP = 256
CHUNK = 128
FFT_N2 = 128
FFT_LANES = 256
FFT_WIDE_MAX_N1 = 16
LANES = 128

V7X_VMEM_BYTES = 64 * 1024 * 1024
BF16_SUBLANES = 16
WORD_SUBLANES = 8
MXU_TILE = 256
MXU_SPLIT_MIN_ROWS = 2 * MXU_TILE

HALO = WORD_SUBLANES
SEQ_TILE = 512
EVEN_TILE = 1024
EVEN_LAG = 3
MAIN_TILE = 1024
MAIN_LAG = 1
PQ_STEP_TILES = 2
PQ_TILE = WORD_SUBLANES * FFT_N2
VMEM_LIMIT = V7X_VMEM_BYTES - 8 * 1024 * 1024

BF16 = jnp.bfloat16
F32 = jnp.float32
U32 = jnp.uint32


def _dot(a, b):
    m = a.shape[0]
    if m >= MXU_SPLIT_MIN_ROWS and m % BF16_SUBLANES == 0:
        h = m // 2 // BF16_SUBLANES * BF16_SUBLANES
        return jnp.concatenate([jnp.dot(a[:h], b, preferred_element_type=F32),
                                jnp.dot(a[h:], b, preferred_element_type=F32)], axis=0)
    return jnp.dot(a, b, preferred_element_type=F32)


def _rms(x, g):
    ms = jnp.mean(x * x, axis=-1, keepdims=True)
    return (x * lax.rsqrt(ms + EPS)) * g


def _silu(z):
    hz = 0.5 * z
    return hz * jnp.tanh(hz) + hz


def _words(x):
    return pltpu.bitcast(x.astype(BF16), U32)


def _rows(w):
    return pltpu.bitcast(w, BF16)


def _store_rows(ref, j, pitch, val):
    ref[pl.ds(j, val.shape[0], stride=pitch), :] = val


def _interleave(streams, lag):
    live = [True] * len(streams)
    step = 0
    while any(live):
        for k, gen in enumerate(streams):
            if live[k] and step >= k * lag:
                try:
                    next(gen)
                except StopIteration:
                    live[k] = False
        step += 1


def _const_spec(shape):
    nd = len(shape)
    return pl.BlockSpec(shape, lambda *_: (0,) * nd, pipeline_mode=pl.Buffered(1))


def _params(sem):
    return pltpu.CompilerParams(dimension_semantics=sem, vmem_limit_bytes=VMEM_LIMIT)


def _even_kernel(xp_ref, x_ref, xn_ref, g_ref, win_ref, cw_ref, pw_ref, ps_ref, wout_ref,
                 fg_ref, *rest, ts, sub, lag, seq, d, final):
    o_ref = rest[0] if final else rest[1]
    i = pl.program_id(1)
    n = pl.num_programs(1)
    g = g_ref[...]
    w_conv = d
    nblk = d // GROUP
    rows = sub + 2 * HALO
    nsub = ts // sub
    hp = jnp.where(i > 0, _rms(xp_ref[0], g), 0.0)
    hn = jnp.where(i < n - 1, _rms(xn_ref[0], g), 0.0)

    def halo(r0):
        return _rms(x_ref[0, r0:r0 + HALO, :], g)

    def stream(k):
        r0 = k * sub
        x = x_ref[0, r0:r0 + sub, :]
        h32 = _rms(x, g)
        hm = h32.astype(BF16)
        hh = jnp.concatenate([hp if k == 0 else halo(r0 - HALO), h32,
                              hn if k == nsub - 1 else halo(r0 + sub)], axis=0).astype(BF16)
        pos = i * ts + r0 + lax.broadcasted_iota(jnp.int32, (sub, 1), 0)
        ys = [None] * (nblk + len(POOL_WINDOWS))

        def conv_dots(cb):
            c0 = cb * GROUP
            return (_dot(hh, win_ref[:, c0:c0 + GROUP]),
                    _dot(hh, win_ref[:, 2 * w_conv + c0:2 * w_conv + c0 + GROUP]),
                    _dot(hm, win_ref[:, w_conv + c0:w_conv + c0 + GROUP]),
                    _dot(hm, win_ref[:, 3 * w_conv + c0:3 * w_conv + c0 + GROUP]))

        def conv_mix(cb, a_h, a_c, a_b, a_z):
            c0 = cb * GROUP
            u = a_c * a_h
            cw = cw_ref[:, c0:c0 + GROUP]
            conv = (pltpu.roll(u, 1, axis=0) * cw[0:1] + u * cw[1:2]
                    + pltpu.roll(u, rows - 1, axis=0) * cw[2:3])
            conv = conv[HALO:HALO + sub]
            ys[cb] = ((a_b * conv) * _silu(a_z)).astype(BF16)

        def pool_dots(grp):
            c0 = 4 * w_conv + grp * GROUP
            return (_dot(hh, win_ref[:, c0:c0 + GROUP]), _dot(hm, win_ref[:, c0 + d:c0 + d + GROUP]))

        def pool_mix(grp, v, b_z):
            win = POOL_WINDOWS[grp]
            q, span = v, 1
            while 2 * span < win:
                q = q + pltpu.roll(q, rows - span, axis=0)
                span *= 2
            s = pltpu.roll(q, span, axis=0) + q
            lo = jnp.maximum(pos - win // 2, 0)
            hi = jnp.minimum(pos + (win - win // 2), seq)
            inv_cnt = 1.0 / (hi - lo).astype(F32)
            dlt = s[HALO:HALO + sub] * inv_cnt - v[HALO:HALO + sub]
            yb = _dot(dlt.astype(BF16), pw_ref[grp].astype(BF16))
            yb = yb * ps_ref[:, grp * GROUP:(grp + 1) * GROUP]
            ys[nblk + grp] = (yb * _silu(b_z)).astype(BF16)

        stages = []
        for blk in range(max(nblk, len(POOL_WINDOWS))):
            if blk < nblk:
                stages.append((conv_dots, conv_mix, blk))
            if blk < len(POOL_WINDOWS):
                stages.append((pool_dots, pool_mix, blk))
        ready = stages[0][0](stages[0][2])
        yield
        for j, (_, mix, blk) in enumerate(stages):
            cur = ready
            if j + 1 < len(stages):
                ready = stages[j + 1][0](stages[j + 1][2])
            mix(blk, *cur)
            yield
        out = x + _dot(jnp.concatenate(ys, axis=1), wout_ref[...])
        if final:
            out = _rms(out, fg_ref[...])
        o_ref[0, r0:r0 + sub, :] = out
        if not final:
            rest[2][0, r0:r0 + sub, :] = _rms(out, rest[0][...]).astype(BF16)
        yield

    _interleave([stream(k) for k in range(nsub)], lag)


def _even_layer(x, g, w_in, conv_w, pool_w, pool_scale, w_out, final_g, final, g_next=None):
    bsz, seq, d = x.shape
    ts = min(EVEN_TILE, seq)
    nt = seq // ts
    hb = ts // HALO
    nhb = seq // HALO
    kern = functools.partial(_even_kernel, ts=ts, sub=SEQ_TILE, lag=EVEN_LAG, seq=seq, d=d, final=final)
    tile = pl.BlockSpec((1, ts, d), lambda b, i: (b, i, 0))
    return pl.pallas_call(
        kern,
        grid=(bsz, nt),
        in_specs=[
            pl.BlockSpec((1, HALO, d), lambda b, i: (b, jnp.maximum(i * hb - 1, 0), 0)),
            pl.BlockSpec((1, ts, d), lambda b, i: (b, i, 0)),
            pl.BlockSpec((1, HALO, d), lambda b, i: (b, jnp.minimum((i + 1) * hb, nhb - 1), 0)),
            _const_spec((1, d)),
            _const_spec(w_in.shape),
            _const_spec(conv_w.shape),
            _const_spec(pool_w.shape),
            _const_spec((1, d)),
            _const_spec(w_out.shape),
            _const_spec((1, d)),
        ] + ([] if final else [_const_spec((1, d))]),
        out_specs=tile if final else [tile, tile],
        out_shape=(jax.ShapeDtypeStruct(x.shape, F32) if final else
                   [jax.ShapeDtypeStruct(x.shape, F32), jax.ShapeDtypeStruct(x.shape, BF16)]),
        compiler_params=_params(("parallel", "arbitrary")),
    )(x, x, x, g.reshape(1, d), w_in, conv_w, pool_w, pool_scale.reshape(1, d), w_out,
      final_g.reshape(1, d), *([] if final else [g_next.reshape(1, d)]))


def _odd_pq_kernel(h_ref, wf_ref, ab_ref, pw_ref, qw_ref, *, d):
    hm = h_ref[0]
    nchunk = h_ref.shape[1] // FFT_N2
    for grp in range(d // GROUP):
        c0 = grp * GROUP
        f = _dot(hm, wf_ref[:, 3 * d + c0:3 * d + c0 + GROUP])
        pq = _dot(f.astype(BF16), ab_ref[grp])
        pw = _words(pq[:, :GROUP])
        qw = _words(pq[:, GROUP:])
        hw = FFT_N2 // 2
        for j in range(nchunk):
            wt, jj = divmod(j, WORD_SUBLANES)
            for l in range(GROUP // LANES):
                sl = grp * (GROUP // LANES) + l
                _store_rows(pw_ref.at[0, sl, wt], jj, WORD_SUBLANES,
                            pw[j * hw:(j + 1) * hw, l * LANES:(l + 1) * LANES])
                _store_rows(qw_ref.at[0, sl, wt], jj, WORD_SUBLANES,
                            qw[j * hw:(j + 1) * hw, l * LANES:(l + 1) * LANES])


def _odd_pq(h, w_f, ab):
    bsz, seq, d = h.shape
    ts = PQ_STEP_TILES * PQ_TILE
    rows = FFT_N2 // 2 * WORD_SUBLANES
    oblk = pl.BlockSpec((1, d // LANES, PQ_STEP_TILES, rows, LANES), lambda b, i: (b, 0, i, 0, 0))
    kern = functools.partial(_odd_pq_kernel, d=d)
    wshape = jax.ShapeDtypeStruct((bsz, d // LANES, seq // PQ_TILE, rows, LANES), U32)
    return pl.pallas_call(
        kern,
        grid=(bsz, seq // ts),
        in_specs=[pl.BlockSpec((1, ts, d), lambda b, i: (b, i, 0)),
                  _const_spec(w_f.shape), _const_spec(ab.shape)],
        out_specs=[oblk, oblk],
        out_shape=[wshape, wshape],
        compiler_params=_params(("parallel", "parallel")),
    )(h, w_f, ab)


def _odd_tile_rows(n):
    tiles = -(-n // WORD_SUBLANES)
    return WORD_SUBLANES * (tiles if tiles % 2 else tiles + 1)


def _fft_kernel(pw_ref, qw_ref, m1_ref, m2_ref, sw_ref, y_scr, *, n1, mp, nwp):
    nw = FFT_N2 // 2
    nslab = pw_ref.shape[0]
    ws = WORD_SUBLANES

    def slabs(get):
        return jnp.concatenate([get(s) for s in range(nslab)], axis=1)

    def t1_rows(ref, s, w2):
        return jnp.concatenate([ref[s, i, w2 * ws:(w2 + 1) * ws, :] for i in range(n1 // ws)], axis=0)

    for w2 in range(nw):
        rhs = jnp.concatenate([slabs(lambda s: _rows(t1_rows(pw_ref, s, w2))),
                               slabs(lambda s: _rows(t1_rows(qw_ref, s, w2)))], axis=0)
        wds = _words(_dot(m1_ref[w2].astype(BF16), rhs))
        for s in range(nslab):
            _store_rows(y_scr.at[s], w2, nwp, wds[:, s * LANES:(s + 1) * LANES])
    m2 = m2_ref[...].astype(BF16)
    for m in range(n1 // 2):
        rhs = jnp.concatenate([slabs(lambda s: _rows(y_scr[s, k * nwp:k * nwp + nw, :]))
                               for k in (2 * m, n1 + 2 * m, 2 * m + 1, n1 + 2 * m + 1)], axis=0)
        wds = _words(_dot(m2, rhs))
        for s in range(nslab):
            _store_rows(sw_ref.at[s], m, mp, wds[:, s * LANES:(s + 1) * LANES])
    if mp > n1 // 2:
        zeros = _words(jnp.zeros((2 * (mp - n1 // 2), LANES), F32))
        for s in range(nslab):
            for k2 in range(FFT_N2):
                sw_ref[s, k2 * mp + n1 // 2:(k2 + 1) * mp, :] = zeros


def _fft(pw, qw, m1, m2, n1):
    bsz, nsl, nt, rows, _ = pw.shape
    nw = FFT_N2 // 2
    per = (FFT_LANES if n1 > FFT_WIDE_MAX_N1 else 2 * FFT_LANES) // LANES
    mp = _odd_tile_rows(n1 // 2)
    nwp = _odd_tile_rows(nw)
    iblk = pl.BlockSpec((None, per, nt, rows, LANES), lambda b, c: (b, c, 0, 0, 0))
    kern = functools.partial(_fft_kernel, n1=n1, mp=mp, nwp=nwp)
    return pl.pallas_call(
        kern,
        grid=(bsz, nsl // per),
        in_specs=[iblk, iblk, _const_spec(m1.shape), _const_spec(m2.shape)],
        out_specs=pl.BlockSpec((None, per, FFT_N2 * mp, LANES), lambda b, c: (b, c, 0, 0)),
        out_shape=jax.ShapeDtypeStruct((bsz, nsl, FFT_N2 * mp, LANES), U32),
        scratch_shapes=[pltpu.VMEM((per, 2 * n1 * nwp, LANES), U32)],
        compiler_params=_params(("parallel", "parallel")),
    )(pw, qw, m1, m2)


def _odd_main_kernel(x_ref, h_ref, sw_ref, win_ref, sg_ref, ws_ref, bs_ref, wout_ref, fg_ref,
                     o_ref, *, ts, sub, lag, d, half_n1, mp, final):
    nhead = d // GROUP
    per = GROUP // LANES
    nk2 = sub // 2 // half_n1

    def stream(k):
        r0 = k * sub
        x = x_ref[0, r0:r0 + sub, :]
        hm = h_ref[0, r0:r0 + sub, :]
        ys = [None] * (2 * nhead)

        def v_proj(hd):
            return _dot(hm, win_ref[:, d + hd * GROUP:d + (hd + 1) * GROUP])

        v_next = v_proj(0)
        yield
        for hd in range(nhead):
            c0 = hd * GROUP
            v = v_next
            c_u = _dot(hm, win_ref[:, c0:c0 + GROUP])
            c_z = _dot(hm, win_ref[:, 2 * d + c0:2 * d + c0 + GROUP])
            d_z = _dot(hm, win_ref[:, 4 * d + c0:4 * d + c0 + GROUP])
            if hd + 1 < nhead:
                v_next = v_proj(hd + 1)
            mu = jnp.mean(v, axis=-1, keepdims=True)
            vc = v - mu
            var = jnp.mean(vc * vc, axis=-1, keepdims=True)
            vn = ((vc * lax.rsqrt(var + EPS)) * sg_ref[:, c0:c0 + GROUP]).astype(BF16)
            ws = ws_ref[hd].astype(BF16)
            mix = jnp.concatenate(
                [_dot(ws, vn[c * CHUNK:(c + 1) * CHUNK]) + bs_ref[hd] for c in range(sub // CHUNK)],
                axis=0)
            ys[hd] = ((c_u * mix) * _silu(c_z)).astype(BF16)
            spec = jnp.concatenate(
                [_rows(sw_ref[0, hd * per + l, k * nk2 * mp:(k + 1) * nk2 * mp, :]
                       .reshape(nk2, mp, LANES)[:, :half_n1].reshape(sub // 2, LANES))
                 for l in range(per)], axis=1).astype(F32)
            ys[nhead + hd] = (spec * _silu(d_z)).astype(BF16)
            yield
        out = x + _dot(jnp.concatenate(ys, axis=1), wout_ref[...])
        if final:
            out = _rms(out, fg_ref[...])
        o_ref[0, r0:r0 + sub, :] = out
        yield

    _interleave([stream(k) for k in range(ts // sub)], lag)


def _odd_main(x, h, sw, w_main, sgu_g, sgu_ws, sgu_bs, w_out, final_g, final):
    bsz, seq, d = x.shape
    ts = min(MAIN_TILE, seq)
    half_n1 = seq // FFT_N2 // 2
    nk2 = ts // 2 // half_n1
    tile = pl.BlockSpec((1, ts, d), lambda b, i: (b, i, 0))
    mp = sw.shape[2] // FFT_N2
    swblk = pl.BlockSpec((1, sw.shape[1], nk2 * mp, LANES), lambda b, i: (b, 0, i, 0))
    kern = functools.partial(_odd_main_kernel, ts=ts, sub=SEQ_TILE, lag=MAIN_LAG, d=d, half_n1=half_n1,
                             mp=mp, final=final)
    return pl.pallas_call(
        kern,
        grid=(bsz, seq // ts),
        in_specs=[tile, tile, swblk,
                  _const_spec(w_main.shape), _const_spec((1, d)), _const_spec(sgu_ws.shape),
                  _const_spec(sgu_bs.shape), _const_spec(w_out.shape), _const_spec((1, d))],
        out_specs=tile,
        out_shape=jax.ShapeDtypeStruct(x.shape, F32),
        compiler_params=_params(("parallel", "parallel")),
    )(x, h, sw, w_main, sgu_g.reshape(1, d), sgu_ws, sgu_bs, w_out,
      final_g.reshape(1, d))


def _fold_kernel(c_ref, s_ref, w_ref, ab_ref):
    for grp in range(w_ref.shape[0]):
        w = w_ref[grp]
        a = jnp.dot(c_ref[...], w, preferred_element_type=F32, precision=lax.Precision.HIGHEST)
        b = jnp.dot(s_ref[...], w, preferred_element_type=F32, precision=lax.Precision.HIGHEST)
        ab_ref[grp, :, :GROUP] = a.astype(BF16)
        ab_ref[grp, :, GROUP:] = b.astype(BF16)


def _fold_fnet(fnet_w):
    ng, dh, _ = fnet_w.shape
    ang = 2.0 * np.pi * np.outer(np.arange(dh), np.arange(dh)) / dh
    scale = dh ** -0.5
    cc = jnp.asarray(np.cos(ang) * scale, F32)
    sc = jnp.asarray(np.sin(ang) * scale, F32)
    return pl.pallas_call(
        _fold_kernel,
        out_shape=jax.ShapeDtypeStruct((ng, dh, 2 * dh), BF16),
    )(cc, sc, fnet_w)


def _fft_tables(seq, n1):
    n2 = seq // n1
    nw = n2 // 2
    w2 = np.arange(nw)[:, None, None, None]
    e = np.arange(2)[None, None, :, None]
    k1 = np.arange(n1)[None, :, None, None]
    t1 = np.arange(n1)[None, None, None, :]
    ang = 2.0 * np.pi * ((k1 * (t1 * n2 + 2 * w2 + e)) % seq) / seq
    c = np.cos(ang) * n1 ** -0.5
    s = np.sin(ang) * n1 ** -0.5
    m1 = np.zeros((nw, 2, n1, 2, 2, n1, 2))
    for ee in range(2):
        m1[:, 0, :, ee, 0, :, ee] = c[:, :, ee, :]
        m1[:, 0, :, ee, 1, :, ee] = -s[:, :, ee, :]
        m1[:, 1, :, ee, 0, :, ee] = -s[:, :, ee, :]
        m1[:, 1, :, ee, 1, :, ee] = -c[:, :, ee, :]
    m1 = m1.reshape(nw, 4 * n1, 4 * n1)
    ang2 = 2.0 * np.pi * (np.outer(np.arange(n2), np.arange(n2)) % n2) / n2
    m2 = np.zeros((n2, 2, 2, 2, n2))
    for ee in range(2):
        m2[:, ee, ee, 0, :] = np.cos(ang2) * n2 ** -0.5
        m2[:, ee, ee, 1, :] = np.sin(ang2) * n2 ** -0.5
    m2 = m2.reshape(2 * n2, 4 * n2)
    return jnp.asarray(m1, F32), jnp.asarray(m2, F32)


def _odd_layer(x, h, w_in, sgu_g, sgu_ws, sgu_bs, ab, w_out, final_g, final):
    bsz, seq, d = x.shape
    n1 = seq // FFT_N2
    m1, m2 = _fft_tables(seq, n1)
    pw, qw = _odd_pq(h, w_in, ab)
    sw = _fft(pw, qw, m1, m2, n1)
    return _odd_main(x, h, sw, w_in, sgu_g, sgu_ws, sgu_bs, w_out, final_g, final)


def _trunk(x, norm_g, ev, od, final_g):
    depth = norm_g.shape[0]
    for i in range(depth):
        j = i // 2
        final = i == depth - 1
        if i % 2 == 0:
            w_in, conv_w, pool_w, pool_scale, w_out = (t[j] for t in ev)
            if final:
                x = _even_layer(x, norm_g[i], w_in, conv_w, pool_w, pool_scale, w_out, final_g, final)
            else:
                x, h = _even_layer(x, norm_g[i], w_in, conv_w, pool_w, pool_scale, w_out, final_g, final,
                                   norm_g[i + 1])
        else:
            w_in, sgu_g, sgu_ws, sgu_bs, ab, w_out = (t[j] for t in od)
            x = _odd_layer(x, h, w_in, sgu_g, sgu_ws, sgu_bs, ab, w_out, final_g, final)
    return x


def kernel(x_prompt, x_sample, norm_g, ev_w_in, ev_conv_w, ev_pool_w, ev_pool_scale, ev_w_out,
           od_w_in, od_sgu_norm_g, od_sgu_ws, od_sgu_bs, od_fnet_w, od_w_out, final_g):
    for x in (x_prompt, x_sample):
        assert x.shape[1] % max(EVEN_TILE, MAIN_TILE, PQ_STEP_TILES * PQ_TILE) == 0, x.shape
        assert x.shape[2] % GROUP == 0 and FFT_LANES % LANES == 0, x.shape
    ev = (ev_w_in.astype(BF16), ev_conv_w, ev_pool_w, ev_pool_scale, ev_w_out.astype(BF16))
    ab = [_fold_fnet(od_fnet_w[j]) for j in range(od_fnet_w.shape[0])]
    od = (od_w_in.astype(BF16), od_sgu_norm_g, od_sgu_ws, od_sgu_bs[..., None], ab,
          od_w_out.astype(BF16))
    y_prompt = _trunk(x_prompt, norm_g, ev, od, final_g)
    y_sample = _trunk(x_sample, norm_g, ev, od, final_g)
    return (y_prompt, y_sample)
```
